```python
import math
import jax, jax.numpy as jnp
from jax import lax
import numpy as np

D_MODEL = 2048
BATCH = 4
SEQ = 2048
DEPTH = 4

GRID_W = 64
CTX_LEN = 256
N_MIXERS = 3
BLOCK_Q = 128
CHUNK = 128
EPS = 1e-6
ROPE_THETA = 10000.0

A_WIDTH = D_MODEL
A_GROUPS = D_MODEL // 128
A_GROUP_DIM = A_WIDTH // A_GROUPS

B_HEAD_DIM = 128
B_HEADS = D_MODEL // B_HEAD_DIM
B_KV_HEADS = B_HEADS // 2
B_GQA_GROUP = B_HEADS // B_KV_HEADS
B_Q_WIDTH = B_HEADS * B_HEAD_DIM
B_KV_WIDTH = B_KV_HEADS * B_HEAD_DIM

C_HEAD_DIM = 64
C_HEADS = D_MODEL // (2 * C_HEAD_DIM)
C_V_DIM = 2 * C_HEAD_DIM
C_WIDTH = C_HEADS * 2 * C_HEAD_DIM

N_GROUPS = 4
EXPERTS_PER_GROUP = 8
TOP_K = 2
D_EXPERT = 3 * D_MODEL // 8
ROUTER_BIAS_SCALE = 0.01

N_GMLP_LAYERS = (DEPTH + N_MIXERS - 1) // N_MIXERS
N_GQA_LAYERS = (DEPTH - 1 + N_MIXERS - 1) // N_MIXERS
N_DIFF_LAYERS = (DEPTH - 2 + N_MIXERS - 1) // N_MIXERS

kernel_name = "hybrid_gmlp_gqa_diffattn_hmoe_dit"


def rms_norm(x, gain):
    xf = x.astype(jnp.float32)
    y = xf * lax.rsqrt(jnp.mean(xf * xf, axis=-1, keepdims=True) + EPS)
    return (y * gain.astype(jnp.float32)).astype(x.dtype)


def modulate(xn, shift, scale):
    return xn * (1 + scale) + shift


def axial_rope_tables(n_tokens, head_dim):
    rows = n_tokens // GRID_W
    row_id = jnp.repeat(jnp.arange(rows, dtype=jnp.float32), GRID_W)
    col_id = jnp.tile(jnp.arange(GRID_W, dtype=jnp.float32), rows)
    half = head_dim // 2
    inv_freq = ROPE_THETA ** (-jnp.arange(0, half, 2, dtype=jnp.float32) / half)
    ang_r = row_id[:, None] * inv_freq[None, :]
    ang_c = col_id[:, None] * inv_freq[None, :]
    ang = jnp.concatenate([ang_r, ang_r, ang_c, ang_c], axis=-1)
    return jnp.cos(ang), jnp.sin(ang)


def apply_axial_rope(x, cos, sin):
    hd = x.shape[-1]
    bshape = (cos.shape[0],) + (1,) * (x.ndim - 3) + (hd,)
    cos = cos.reshape(bshape).astype(x.dtype)
    sin = sin.reshape(bshape).astype(x.dtype)
    r1, r2, c1, c2 = jnp.split(x, 4, axis=-1)
    rot = jnp.concatenate([-r2, r1, -c2, c1], axis=-1)
    return x * cos + rot * sin


def sweep_query_blocks(fn, q):
    bsz, n = q.shape[:2]
    qb = q.reshape((bsz, n // BLOCK_Q, BLOCK_Q) + q.shape[2:]).swapaxes(0, 1)
    ob = lax.map(fn, qb)
    ob = ob.swapaxes(0, 1)
    return ob.reshape((bsz, n) + ob.shape[3:])


def chunk_spatial_gating(h, w_in, v_gain, w_s, b_s):
    bsz, n, _ = h.shape
    z = jax.nn.gelu(h @ w_in)
    u, v = jnp.split(z, 2, axis=-1)
    v = rms_norm(v, v_gain).reshape(bsz, n // CHUNK, CHUNK, A_GROUPS, A_GROUP_DIM)
    s = jnp.einsum('gpq,bcqge->bcpge', w_s, v) + b_s.T[:, :, None]
    return u * s.reshape(bsz, n, A_WIDTH)


def mixer_gmlp(hc, hx, w_in, v_gain, w_s, b_s, w_out, need_ctx):
    yx = chunk_spatial_gating(hx, w_in, v_gain, w_s, b_s) @ w_out
    yc = chunk_spatial_gating(hc, w_in, v_gain, w_s, b_s) @ w_out if need_ctx else None
    return yc, yx


def gqa_softmax(q, k, v):
    s = jnp.einsum('bqhgd,bkhd->bhgqk', q, k).astype(jnp.float32) / math.sqrt(q.shape[-1])
    p = jax.nn.softmax(s, axis=-1).astype(v.dtype)
    return jnp.einsum('bhgqk,bkhd->bqhgd', p, v)


def mixer_gqa(hc, hx, w_qkv, q_gain, k_gain, w_o, cos, sin, need_ctx):
    def project(h):
        bsz, n, _ = h.shape
        q, k, v = jnp.split(h @ w_qkv, [B_Q_WIDTH, B_Q_WIDTH + B_KV_WIDTH], axis=-1)
        q = rms_norm(q.reshape(bsz, n, B_KV_HEADS, B_GQA_GROUP, B_HEAD_DIM), q_gain)
        k = rms_norm(k.reshape(bsz, n, B_KV_HEADS, B_HEAD_DIM), k_gain)
        v = v.reshape(bsz, n, B_KV_HEADS, B_HEAD_DIM)
        return q, k, v

    qc, kc, vc = project(hc)
    qx, kx, vx = project(hx)
    qx = apply_axial_rope(qx, cos, sin)
    kx = apply_axial_rope(kx, cos, sin)
    k_all = jnp.concatenate([kc, kx], axis=1)
    v_all = jnp.concatenate([vc, vx], axis=1)
    ox = sweep_query_blocks(lambda qb: gqa_softmax(qb, k_all, v_all), qx)
    yx = ox.reshape(hx.shape[0], hx.shape[1], B_Q_WIDTH) @ w_o
    yc = None
    if need_ctx:
        oc = gqa_softmax(qc, kc, vc)
        yc = oc.reshape(hc.shape[0], hc.shape[1], B_Q_WIDTH) @ w_o
    return yc, yx


def mixer_diff(hc, hx, w_qkv, q_gain, k_gain, lam_q1, lam_k1, lam_q2, lam_k2, sub_gain, w_o,
               cos, sin, lam_init, need_ctx):
    def project(h):
        bsz, n, _ = h.shape
        q, k, v = jnp.split(h @ w_qkv, 3, axis=-1)
        q = rms_norm(q.reshape(bsz, n, C_HEADS, 2, C_HEAD_DIM), q_gain)
        k = rms_norm(k.reshape(bsz, n, C_HEADS, 2, C_HEAD_DIM), k_gain)
        return q, k, v.reshape(bsz, n, C_HEADS, C_V_DIM)

    f32 = jnp.float32
    lam = (jnp.exp(jnp.sum(lam_q1.astype(f32) * lam_k1.astype(f32)))
           - jnp.exp(jnp.sum(lam_q2.astype(f32) * lam_k2.astype(f32))) + lam_init)

    def attend(q, k, v):
        s = jnp.einsum('bqhmd,bkhmd->bhmqk', q, k).astype(f32) / math.sqrt(C_HEAD_DIM)
        p = jax.nn.softmax(s, axis=-1)
        a = (p[:, :, 0] - lam * p[:, :, 1]).astype(v.dtype)
        return jnp.einsum('bhqk,bkhe->bqhe', a, v)

    def finish(o):
        o = rms_norm(o, sub_gain) * (1.0 - lam_init)
        return o.reshape(o.shape[0], o.shape[1], C_HEADS * C_V_DIM) @ w_o

    qc, kc, vc = project(hc)
    qx, kx, vx = project(hx)
    qx = apply_axial_rope(qx, cos, sin)
    kx = apply_axial_rope(kx, cos, sin)
    k_all = jnp.concatenate([kc, kx], axis=1)
    v_all = jnp.concatenate([vc, vx], axis=1)
    yx = finish(sweep_query_blocks(lambda qb: attend(qb, k_all, v_all), qx))
    yc = finish(attend(qc, kc, vc)) if need_ctx else None
    return yc, yx


def hierarchical_moe(t, w_grp, b_grp, w_exp, b_exp, w_gate, w_up, w_down):
    f32 = jnp.float32
    n_tok = t.shape[0]
    grp_logits = (t @ w_grp).astype(f32) + b_grp.astype(f32)
    grp_prob = jax.nn.softmax(grp_logits, axis=-1)
    grp_idx = jnp.argmax(grp_logits, axis=-1)
    grp_onehot = jax.nn.one_hot(grp_idx, N_GROUPS, dtype=f32)
    grp_w = jnp.sum(grp_prob * grp_onehot, axis=-1)
    exp_logits = ((t @ w_exp).astype(f32) + b_exp.astype(f32)).reshape(n_tok, N_GROUPS, EXPERTS_PER_GROUP)
    sel_logits = jnp.take_along_axis(exp_logits, grp_idx[:, None, None], axis=1)[:, 0]
    top_v, top_i = lax.top_k(sel_logits, TOP_K)
    top_w = jax.nn.softmax(top_v, axis=-1)
    in_grp = jnp.sum(jax.nn.one_hot(top_i, EXPERTS_PER_GROUP, dtype=f32) * top_w[..., None], axis=1)
    combine = (grp_w[:, None, None] * grp_onehot[:, :, None] * in_grp[:, None, :]).astype(t.dtype)
    out = jnp.zeros_like(t)
    for g in range(N_GROUPS):
        a = jnp.einsum('td,edf->tef', t, w_gate[g])
        b = jnp.einsum('td,edf->tef', t, w_up[g])
        out = out + jnp.einsum('tef,efd->td', jax.nn.silu(a) * b * combine[:, g, :, None], w_down[g])
    return out


def setup_inputs(seed: int = 0) -> dict:
    key = jax.random.key(seed)
    ks = iter(jax.random.split(key, 64))
    d = D_MODEL

    def nrm(shape, scale):
        return jax.random.normal(next(ks), shape, jnp.float32) * scale

    def gain(shape):
        return 1.0 + nrm(shape, 0.05)

    return {
        "x": nrm((BATCH, SEQ, d), 1.0),
        "c": nrm((BATCH, d), 1.0),
        "ctx": nrm((BATCH, CTX_LEN, d), 1.0),
        "c_ctx": nrm((d,), 1.0),
        "ada_w": nrm((DEPTH, d, 6 * d), 0.5 * d ** -0.5),
        "ada_b": nrm((DEPTH, 6 * d), 0.02),
        "norm1_g": gain((DEPTH, d)),
        "norm2_g": gain((DEPTH, d)),
        "gmlp_w_in": nrm((N_GMLP_LAYERS, d, 2 * A_WIDTH), d ** -0.5),
        "gmlp_v_gain": gain((N_GMLP_LAYERS, A_WIDTH)),
        "gmlp_w_s": nrm((N_GMLP_LAYERS, A_GROUPS, CHUNK, CHUNK), CHUNK ** -0.5),
        "gmlp_b_s": gain((N_GMLP_LAYERS, A_GROUPS, CHUNK)),
        "gmlp_w_out": nrm((N_GMLP_LAYERS, A_WIDTH, d), A_WIDTH ** -0.5),
        "gqa_w_qkv": nrm((N_GQA_LAYERS, d, B_Q_WIDTH + 2 * B_KV_WIDTH), d ** -0.5),
        "gqa_q_gain": gain((N_GQA_LAYERS, B_HEAD_DIM)),
        "gqa_k_gain": gain((N_GQA_LAYERS, B_HEAD_DIM)),
        "gqa_w_o": nrm((N_GQA_LAYERS, B_Q_WIDTH, d), B_Q_WIDTH ** -0.5),
        "diff_w_qkv": nrm((N_DIFF_LAYERS, d, 3 * C_WIDTH), d ** -0.5),
        "diff_q_gain": gain((N_DIFF_LAYERS, 2, C_HEAD_DIM)),
        "diff_k_gain": gain((N_DIFF_LAYERS, 2, C_HEAD_DIM)),
        "diff_lam_q1": nrm((N_DIFF_LAYERS, C_HEAD_DIM), 0.1),
        "diff_lam_k1": nrm((N_DIFF_LAYERS, C_HEAD_DIM), 0.1),
        "diff_lam_q2": nrm((N_DIFF_LAYERS, C_HEAD_DIM), 0.1),
        "diff_lam_k2": nrm((N_DIFF_LAYERS, C_HEAD_DIM), 0.1),
        "diff_sub_gain": gain((N_DIFF_LAYERS, C_V_DIM)),
        "diff_w_o": nrm((N_DIFF_LAYERS, C_HEADS * C_V_DIM, d), (C_HEADS * C_V_DIM) ** -0.5),
        "moe_w_grp": nrm((DEPTH, d, N_GROUPS), d ** -0.5),
        "moe_b_grp": nrm((DEPTH, N_GROUPS), ROUTER_BIAS_SCALE),
        "moe_w_exp": nrm((DEPTH, d, N_GROUPS * EXPERTS_PER_GROUP), d ** -0.5),
        "moe_b_exp": nrm((DEPTH, N_GROUPS * EXPERTS_PER_GROUP), ROUTER_BIAS_SCALE),
        "moe_w_gate": nrm((DEPTH, N_GROUPS, EXPERTS_PER_GROUP, d, D_EXPERT), d ** -0.5),
        "moe_w_up": nrm((DEPTH, N_GROUPS, EXPERTS_PER_GROUP, d, D_EXPERT), d ** -0.5),
        "moe_w_down": nrm((DEPTH, N_GROUPS, EXPERTS_PER_GROUP, D_EXPERT, d), D_EXPERT ** -0.5),
    }


def reference(x, c, ctx, c_ctx, ada_w, ada_b, norm1_g, norm2_g,
              gmlp_w_in, gmlp_v_gain, gmlp_w_s, gmlp_b_s, gmlp_w_out,
              gqa_w_qkv, gqa_q_gain, gqa_k_gain, gqa_w_o,
              diff_w_qkv, diff_q_gain, diff_k_gain, diff_lam_q1, diff_lam_k1, diff_lam_q2, diff_lam_k2,
              diff_sub_gain, diff_w_o,
              moe_w_grp, moe_b_grp, moe_w_exp, moe_b_exp, moe_w_gate, moe_w_up, moe_w_down):
    bsz, n_lat, d = x.shape
    n_ctx = ctx.shape[1]
    cos_b, sin_b = axial_rope_tables(n_lat, B_HEAD_DIM)
    cos_c, sin_c = axial_rope_tables(n_lat, C_HEAD_DIM)
    hx, hc = x, ctx
    for i in range(DEPTH):
        kind = i % N_MIXERS
        j = i // N_MIXERS
        need_ctx = any((k % N_MIXERS) != 0 for k in range(i + 1, DEPTH))
        sh1, sc1, g1, sh2, sc2, g2 = jnp.split(jax.nn.silu(c) @ ada_w[i] + ada_b[i], 6, axis=-1)
        csh1, csc1, cg1, csh2, csc2, cg2 = jnp.split(jax.nn.silu(c_ctx) @ ada_w[i] + ada_b[i], 6, axis=-1)

        ax = modulate(rms_norm(hx, norm1_g[i]), sh1[:, None], sc1[:, None])
        ac = modulate(rms_norm(hc, norm1_g[i]), csh1, csc1)
        if kind == 0:
            yc, yx = mixer_gmlp(ac, ax, gmlp_w_in[j], gmlp_v_gain[j], gmlp_w_s[j], gmlp_b_s[j],
                                gmlp_w_out[j], need_ctx)
        elif kind == 1:
            yc, yx = mixer_gqa(ac, ax, gqa_w_qkv[j], gqa_q_gain[j], gqa_k_gain[j], gqa_w_o[j],
                               cos_b, sin_b, need_ctx)
        else:
            lam_init = 0.8 - 0.6 * math.exp(-0.3 * i)
            yc, yx = mixer_diff(ac, ax, diff_w_qkv[j], diff_q_gain[j], diff_k_gain[j],
                                diff_lam_q1[j], diff_lam_k1[j], diff_lam_q2[j], diff_lam_k2[j],
                                diff_sub_gain[j], diff_w_o[j], cos_c, sin_c, lam_init, need_ctx)
        hx = hx + g1[:, None] * yx

        mx = modulate(rms_norm(hx, norm2_g[i]), sh2[:, None], sc2[:, None])
        moe_args = (moe_w_grp[i], moe_b_grp[i], moe_w_exp[i], moe_b_exp[i],
                    moe_w_gate[i], moe_w_up[i], moe_w_down[i])
        if need_ctx:
            hc = hc + cg1 * yc
            mc = modulate(rms_norm(hc, norm2_g[i]), csh2, csc2)
            tokens = jnp.concatenate([mc, mx], axis=1).reshape(-1, d)
            y = hierarchical_moe(tokens, *moe_args).reshape(bsz, n_ctx + n_lat, d)
            hc = hc + cg2 * y[:, :n_ctx]
            yx2 = y[:, n_ctx:]
        else:
            yx2 = hierarchical_moe(mx.reshape(-1, d), *moe_args).reshape(bsz, n_lat, d)
        hx = hx + g2[:, None] * yx2
    return hx
```

```python
import functools
import math

import jax
import jax.numpy as jnp
from jax import lax
from jax.experimental import pallas as pl
from jax.experimental.pallas import tpu as pltpu

EPS = 1e-6
GRID_W = 64
ROPE_THETA = 10000.0
N_MIXERS = 3
CHUNK = 128
LANES = 128
HEAD_LANES = 128
MXU_COLS = 256
MOE_ROW_TILE = 128
VMEM_LIMIT_BYTES = 60000 * 1024
ADA_COL_TILE = 1024

F32 = jnp.float32
BF16 = jnp.bfloat16


def _cparams(*sem):
    return pltpu.CompilerParams(dimension_semantics=sem, vmem_limit_bytes=VMEM_LIMIT_BYTES)


def _resident(shape):
    nd = len(shape)
    return pl.BlockSpec(shape, lambda *_: (0,) * nd, pipeline_mode=pl.Buffered(1))


def _full(shape):
    nd = len(shape)
    return pl.BlockSpec(shape, lambda *_: (0,) * nd)


def _norm_mod(x, g, shift, scale):
    y = x * lax.rsqrt(jnp.mean(x * x, axis=-1, keepdims=True) + EPS)
    return (y * g) * (1.0 + scale) + shift


def _dot(a, b):
    return jnp.dot(a, b, preferred_element_type=F32)


def _ada_kernel(cc_ref, w_ref, b_ref, o_ref):
    s = jax.nn.silu(cc_ref[...])
    o_ref[...] = jnp.dot(s, w_ref[...], preferred_element_type=F32,
                         precision=lax.Precision.HIGHEST) + b_ref[...]


def _ada_table(cc, ada_w, ada_b):
    depth, d, n6 = ada_w.shape
    rows = cc.shape[0]
    tn = ADA_COL_TILE
    return pl.pallas_call(
        _ada_kernel,
        grid=(depth, n6 // tn),
        in_specs=[
            pl.BlockSpec((rows, d), lambda l, n: (0, 0)),
            pl.BlockSpec((None, d, tn), lambda l, n: (l, 0, n)),
            pl.BlockSpec((None, 1, tn), lambda l, n: (l, 0, n)),
        ],
        out_specs=pl.BlockSpec((None, rows, tn), lambda l, n: (l, 0, n)),
        out_shape=jax.ShapeDtypeStruct((depth, rows, n6), F32),
        compiler_params=_cparams("arbitrary", "arbitrary"),
        name="ada_table",
    )(cc, ada_w, ada_b.reshape(depth, 1, n6))


class _Tiles:
    def __init__(self, n_batch, tm, ctx_tiles):
        self.n_batch, self.tm, self.ctx_tiles = n_batch, tm, ctx_tiles

    def mod(self, j, d):
        nb, ct = self.n_batch, self.ctx_tiles
        return pl.BlockSpec((None, None, 1, d), lambda b, t: (j, jnp.where(t < ct, nb, b), 0, 0))

    def rows(self, d, off=0):
        tm = self.tm
        return pl.BlockSpec((None, tm, d), lambda b, t: (b, t + off, 0))


def _gmlp_kernel(h_ref, sh_ref, sc_ref, gt_ref, g_ref, win_ref, vg_ref, ws_ref, bs_ref, wout_ref,
                 o_ref, gated_ref, *, width, groups):
    x = h_ref[...]
    tm = x.shape[0]
    a = _norm_mod(x, g_ref[...], sh_ref[...], sc_ref[...]).astype(BF16)
    v = jax.nn.gelu(_dot(a, win_ref[:, width:]))
    v = v * lax.rsqrt(jnp.mean(v * v, axis=-1, keepdims=True) + EPS) * vg_ref[...]
    v = v.astype(BF16)
    gd = width // groups
    per = MXU_COLS // gd
    for g0 in range(0, groups, per):
        u = jax.nn.gelu(_dot(a, win_ref[:, g0 * gd:(g0 + per) * gd]))
        for g in range(g0, g0 + per):
            cols = slice(g * gd, (g + 1) * gd)
            ug = u[:, (g - g0) * gd:(g - g0 + 1) * gd]
            for c in range(tm // CHUNK):
                rows = slice(c * CHUNK, (c + 1) * CHUNK)
                s = _dot(ws_ref[g], v[rows, cols]) + bs_ref[:, g:g + 1]
                gated_ref[rows, cols] = (ug[rows] * s).astype(BF16)
    y = _dot(gated_ref[...], wout_ref[...])
    o_ref[...] = x + gt_ref[...] * y


def _gmlp_layer(h, modl, g1, w_in, v_gain, w_s, b_s, w_out, tiles, n_tiles, off):
    n_batch, _, d = h.shape
    tm = tiles.tm
    width = w_out.shape[0]
    groups = w_s.shape[0]
    kern = functools.partial(_gmlp_kernel, width=width, groups=groups)
    return pl.pallas_call(
        kern,
        grid=(n_batch, n_tiles),
        in_specs=[
            tiles.rows(d, off), tiles.mod(0, d), tiles.mod(1, d), tiles.mod(2, d),
            _full((1, d)), _resident(w_in.shape), _full((1, width)), _resident(w_s.shape),
            _full(b_s.shape), _resident(w_out.shape),
        ],
        out_specs=tiles.rows(d),
        out_shape=jax.ShapeDtypeStruct((n_batch, n_tiles * tm, d), F32),
        scratch_shapes=[pltpu.VMEM((tm, width), BF16)],
        compiler_params=_cparams("arbitrary", "arbitrary"),
        name="gmlp_mixer",
    )(h, modl, modl, modl, g1, w_in, v_gain, w_s, b_s, w_out)


def _rope(x, cos, s_dn, s_up, quarter):
    return (x * cos + pltpu.roll(x, LANES - quarter, 1) * s_dn + pltpu.roll(x, quarter, 1) * s_up)


def _qkv_kernel(h_ref, sh_ref, sc_ref, g_ref, w_ref, qg_ref, kg_ref, cos_ref, sdn_ref, sup_ref,
                q_ref, k_ref, v_ref, *, n_q, n_k, n_v, halves, quarter, q_scale):
    x = h_ref[...]
    a = _norm_mod(x, g_ref[...], sh_ref[...], sc_ref[...]).astype(BF16)
    cos, sdn, sup = cos_ref[...], sdn_ref[...], sup_ref[...]
    lane = lax.broadcasted_iota(jnp.int32, (1, HEAD_LANES), 1)
    lo = lane < (HEAD_LANES // 2)

    def head_norm(t, gain):
        sq = t * t
        if halves == 1:
            inv = lax.rsqrt(jnp.mean(sq, axis=-1, keepdims=True) + EPS)
        else:
            tot = jnp.sum(sq, axis=-1, keepdims=True)
            s_lo = jnp.sum(jnp.where(lo, sq, 0.0), axis=-1, keepdims=True)
            half = HEAD_LANES // 2
            inv = jnp.where(lo, lax.rsqrt(s_lo / half + EPS), lax.rsqrt((tot - s_lo) / half + EPS))
        return t * inv * gain

    qg, kg = qg_ref[...], kg_ref[...]
    per = MXU_COLS // HEAD_LANES
    for j0 in range(0, n_q + n_k + n_v, per):
        t2 = _dot(a, w_ref[:, j0 * HEAD_LANES:(j0 + per) * HEAD_LANES])
        for j in range(j0, j0 + per):
            t = t2[:, (j - j0) * HEAD_LANES:(j - j0 + 1) * HEAD_LANES]
            if j < n_q:
                t = _rope(head_norm(t, qg), cos, sdn, sup, quarter) * q_scale
                q_ref[j] = t.astype(BF16)
            elif j < n_q + n_k:
                k_ref[j - n_q] = _rope(head_norm(t, kg), cos, sdn, sup, quarter).astype(BF16)
            else:
                v_ref[j - n_q - n_k] = t.astype(BF16)


def _qkv_proj(h, modl, g1, w, q_gain, k_gain, rope_tabs, tiles, n_tiles, *, n_q, n_k, n_v, halves,
              quarter, q_scale):
    n_batch, n_rows, d = h.shape
    tm = tiles.tm
    cos, sdn, sup = rope_tabs
    kern = functools.partial(_qkv_kernel, n_q=n_q, n_k=n_k, n_v=n_v, halves=halves,
                             quarter=quarter, q_scale=q_scale)
    tab = pl.BlockSpec((tm, HEAD_LANES), lambda b, t: (t, 0))

    def heads(n):
        return pl.BlockSpec((None, n, tm, HEAD_LANES), lambda b, t: (b, 0, t, 0))

    def out(n):
        return jax.ShapeDtypeStruct((n_batch, n, n_rows, HEAD_LANES), BF16)

    return pl.pallas_call(
        kern,
        grid=(n_batch, n_tiles),
        in_specs=[
            tiles.rows(d), tiles.mod(0, d), tiles.mod(1, d), _full((1, d)), _resident(w.shape),
            _full((1, HEAD_LANES)), _full((1, HEAD_LANES)), tab, tab, tab,
        ],
        out_specs=[heads(n_q), heads(n_k), heads(n_v)],
        out_shape=[out(n_q), out(n_k), out(n_v)],
        compiler_params=_cparams("arbitrary", "arbitrary"),
        name="qkv_proj",
    )(h, modl, modl, g1, w, q_gain, k_gain, cos, sdn, sup)


def _softmax_pv(q, k, v):
    s = lax.dot_general(q, k, (((1,), (1,)), ((), ())), preferred_element_type=F32)
    p = jnp.exp(s - jnp.max(s, axis=-1, keepdims=True))
    o = _dot(p.astype(BF16), v)
    return o / jnp.sum(p, axis=-1, keepdims=True)


def _gqa_attn_kernel(q_ref, k_ref, v_ref, o_ref, *, group, n_ctx, ctx_tiles):
    qi = pl.program_id(2)

    def attend(n_keys):
        k, v = k_ref[:n_keys], v_ref[:n_keys]
        for g in range(group):
            o = _softmax_pv(q_ref[g], k, v)
            o_ref[:, g * HEAD_LANES:(g + 1) * HEAD_LANES] = o.astype(BF16)

    if ctx_tiles:
        pl.when(qi < ctx_tiles)(lambda: attend(n_ctx))
        pl.when(qi >= ctx_tiles)(lambda: attend(k_ref.shape[0]))
    else:
        attend(k_ref.shape[0])


def _gqa_attention(q, k, v, tq, n_ctx, q_tiles, q_off):
    n_batch, n_heads, n_rows, _ = q.shape
    kv_heads = k.shape[1]
    group = n_heads // kv_heads
    ctx_tiles = 0 if q_off else n_ctx // tq
    kern = functools.partial(_gqa_attn_kernel, group=group, n_ctx=n_ctx, ctx_tiles=ctx_tiles)
    kv = pl.BlockSpec((None, None, n_rows, HEAD_LANES), lambda b, h, t: (b, h, 0, 0))
    return pl.pallas_call(
        kern,
        grid=(n_batch, kv_heads, q_tiles),
        in_specs=[
            pl.BlockSpec((None, group, tq, HEAD_LANES), lambda b, h, t: (b, h, t + q_off, 0)),
            kv, kv,
        ],
        out_specs=pl.BlockSpec((None, tq, group * HEAD_LANES), lambda b, h, t: (b, t, h)),
        out_shape=jax.ShapeDtypeStruct((n_batch, q_tiles * tq, n_heads * HEAD_LANES), BF16),
        compiler_params=_cparams("arbitrary", "arbitrary", "arbitrary"),
        name="gqa_attention",
    )(q, k, v)


def _diff_attn_kernel(q_ref, k_ref, v_ref, lq1_ref, lk1_ref, lq2_ref, lk2_ref, sg_ref, o_ref, *,
                      lam_init, n_ctx, ctx_tiles):
    qi = pl.program_id(2)
    lam = (jnp.exp(jnp.sum(lq1_ref[...] * lk1_ref[...], axis=-1, keepdims=True))
           - jnp.exp(jnp.sum(lq2_ref[...] * lk2_ref[...], axis=-1, keepdims=True)) + lam_init)
    lane = lax.broadcasted_iota(jnp.int32, (1, HEAD_LANES), 1)
    lo = lane < (HEAD_LANES // 2)

    def attend(n_keys):
        q, k, v = q_ref[...], k_ref[:n_keys], v_ref[:n_keys]
        zero = jnp.zeros_like(q)
        dims = (((1,), (1,)), ((), ()))
        s0 = lax.dot_general(jnp.where(lo, q, zero), k, dims, preferred_element_type=F32)
        s1 = lax.dot_general(jnp.where(lo, zero, q), k, dims, preferred_element_type=F32)
        e0 = jnp.exp(s0 - jnp.max(s0, axis=-1, keepdims=True))
        e1 = jnp.exp(s1 - jnp.max(s1, axis=-1, keepdims=True))
        c0 = 1.0 / jnp.sum(e0, axis=-1, keepdims=True)
        c1 = lam / jnp.sum(e1, axis=-1, keepdims=True)
        o = _dot((e0 * c0 - e1 * c1).astype(BF16), v)
        o = o * lax.rsqrt(jnp.mean(o * o, axis=-1, keepdims=True) + EPS) * sg_ref[...]
        o_ref[...] = (o * (1.0 - lam_init)).astype(BF16)

    if ctx_tiles:
        pl.when(qi < ctx_tiles)(lambda: attend(n_ctx))
        pl.when(qi >= ctx_tiles)(lambda: attend(k_ref.shape[0]))
    else:
        attend(k_ref.shape[0])


def _diff_attention(q, k, v, lam_vecs, sub_gain, lam_init, tq, n_ctx, q_tiles, q_off):
    n_batch, n_heads, n_rows, _ = q.shape
    ctx_tiles = 0 if q_off else n_ctx // tq
    kern = functools.partial(_diff_attn_kernel, lam_init=lam_init, n_ctx=n_ctx, ctx_tiles=ctx_tiles)
    kv = pl.BlockSpec((None, None, n_rows, HEAD_LANES), lambda b, h, t: (b, h, 0, 0))
    vec = _full(lam_vecs[0].shape)
    return pl.pallas_call(
        kern,
        grid=(n_batch, n_heads, q_tiles),
        in_specs=[
            pl.BlockSpec((None, None, tq, HEAD_LANES), lambda b, h, t: (b, h, t + q_off, 0)),
            kv, kv, vec, vec, vec, vec, _full((1, HEAD_LANES)),
        ],
        out_specs=pl.BlockSpec((None, tq, HEAD_LANES), lambda b, h, t: (b, t, h)),
        out_shape=jax.ShapeDtypeStruct((n_batch, q_tiles * tq, n_heads * HEAD_LANES), BF16),
        compiler_params=_cparams("arbitrary", "arbitrary", "arbitrary"),
        name="diff_attention",
    )(q, k, v, *lam_vecs, sub_gain)


def _oproj_kernel(o_ref, w_ref, h_ref, gt_ref, out_ref):
    out_ref[...] = h_ref[...] + gt_ref[...] * _dot(o_ref[...], w_ref[...])


def _out_proj(o, w, h, modl, tiles, n_tiles, h_off):
    n_batch, _, d = h.shape
    tm = tiles.tm
    return pl.pallas_call(
        _oproj_kernel,
        grid=(n_batch, n_tiles),
        in_specs=[tiles.rows(o.shape[-1]), _resident(w.shape), tiles.rows(d, h_off), tiles.mod(2, d)],
        out_specs=tiles.rows(d),
        out_shape=jax.ShapeDtypeStruct((n_batch, n_tiles * tm, d), F32),
        compiler_params=_cparams("arbitrary", "arbitrary"),
        name="out_proj",
    )(o, w, h, modl)


def _router_kernel(h_ref, sh_ref, sc_ref, g_ref, wr_ref, br_ref, mx_ref, ei_ref, wf_ref, cnt_ref,
                   carry_ref, *, n_groups, n_experts):
    first = jnp.logical_and(pl.program_id(0) == 0, pl.program_id(1) == 0)

    @pl.when(first)
    def _():
        carry_ref[...] = jnp.zeros_like(carry_ref)

    x = h_ref[...]
    tm = x.shape[0]
    mx = _norm_mod(x, g_ref[...], sh_ref[...], sc_ref[...])
    mx_ref[...] = mx
    logits = jnp.dot(mx, wr_ref[...], preferred_element_type=F32,
                     precision=lax.Precision.HIGHEST) + br_ref[...]
    lane = lax.broadcasted_iota(jnp.int32, logits.shape, 1).astype(F32)
    neg = jnp.float32(-jnp.inf)
    big = jnp.float32(LANES)

    def first_argmax(vals):
        m = jnp.max(vals, axis=-1, keepdims=True)
        return m, jnp.min(jnp.where(vals == m, lane, big), axis=-1, keepdims=True)

    glog = jnp.where(lane < n_groups, logits, neg)
    gmax, gidx = first_argmax(glog)
    grp_w = 1.0 / jnp.sum(jnp.exp(glog - gmax), axis=-1, keepdims=True)
    e_lo = n_groups + gidx * n_experts
    elog = jnp.where(jnp.logical_and(lane >= e_lo, lane < e_lo + n_experts), logits, neg)
    v1, i1 = first_argmax(elog)
    v2, i2 = first_argmax(jnp.where(lane == i1, neg, elog))
    t = jnp.exp(v2 - v1)
    w1 = grp_w / (1.0 + t)
    w2 = grp_w * t / (1.0 + t)

    onehot = jnp.logical_or(lane == i1, lane == i2)
    r_io = lax.broadcasted_iota(jnp.int32, (tm, tm), 0)
    c_io = lax.broadcasted_iota(jnp.int32, (tm, tm), 1)
    tri = jnp.where(r_io >= c_io, 1.0, 0.0).astype(BF16)
    incl = _dot(tri, jnp.where(onehot, 1.0, 0.0).astype(BF16))
    before = carry_ref[...] + incl - 1.0
    r1 = jnp.sum(jnp.where(lane == i1, before, 0.0), axis=-1, keepdims=True)
    r2 = jnp.sum(jnp.where(lane == i2, before, 0.0), axis=-1, keepdims=True)
    carry = carry_ref[...] + incl[tm - 1:tm, :]
    carry_ref[...] = carry
    cnt_ref[...] = carry.astype(jnp.int32)

    ei = jnp.where(lane == 0, i1 - n_groups,
                   jnp.where(lane == 1, i2 - n_groups,
                             jnp.where(lane == 2, r1, jnp.where(lane == 3, r2, 0.0))))
    ei_ref[...] = ei.astype(jnp.int32)
    wf_ref[...] = jnp.where(lane == 0, w1, jnp.where(lane == 1, w2, 0.0))


def _norm_router(h, modl, g2, w_r, b_r, tiles, n_tiles, off, n_groups, n_experts):
    n_batch, _, d = h.shape
    tm = tiles.tm
    n_tok = n_batch * n_tiles * tm
    kern = functools.partial(_router_kernel, n_groups=n_groups, n_experts=n_experts)

    def flat(width):
        return pl.BlockSpec((tm, width), lambda b, t: (b * n_tiles + t, 0))

    return pl.pallas_call(
        kern,
        grid=(n_batch, n_tiles),
        in_specs=[tiles.rows(d, off), tiles.mod(3, d), tiles.mod(4, d), _full((1, d)),
                  _full(w_r.shape), _full((1, LANES))],
        out_specs=[flat(d), flat(LANES), flat(LANES), _full((1, LANES))],
        out_shape=[
            jax.ShapeDtypeStruct((n_tok, d), F32),
            jax.ShapeDtypeStruct((n_tok, LANES), jnp.int32),
            jax.ShapeDtypeStruct((n_tok, LANES), F32),
            jax.ShapeDtypeStruct((1, LANES), jnp.int32),
        ],
        scratch_shapes=[pltpu.VMEM((1, LANES), F32)],
        compiler_params=_cparams("arbitrary", "arbitrary"),
        name="moe_norm_router",
    )(h, modl, modl, g2, w_r, b_r)


def _row_copy(src_ref, src_row, dst_ref, dst_row, sem):
    return pltpu.make_async_copy(src_ref.at[pl.ds(src_row, 1), :], dst_ref.at[pl.ds(dst_row, 1), :], sem)


def _dispatch_kernel(pos_ref, mx_ref, xs_in_ref, xs_ref, sem, *, n_tok, top_k):
    del xs_in_ref
    tm = mx_ref.shape[0]
    base = pl.program_id(0) * tm

    def start(r, carry):
        for kk in range(top_k):
            _row_copy(mx_ref, r, xs_ref, pos_ref[kk * n_tok + base + r], sem).start()
        return carry

    lax.fori_loop(0, tm, start, 0, unroll=8)

    def wait(r, carry):
        for kk in range(top_k):
            _row_copy(mx_ref, 0, xs_ref, 0, sem).wait()
        return carry

    lax.fori_loop(0, tm, wait, 0, unroll=8)


def _dispatch(pos, mx, n_rows, tm, top_k):
    n_tok, d = mx.shape
    kern = functools.partial(_dispatch_kernel, n_tok=n_tok, top_k=top_k)
    xs0 = jnp.zeros((n_rows, d), F32)
    return pl.pallas_call(
        kern,
        grid_spec=pltpu.PrefetchScalarGridSpec(
            num_scalar_prefetch=1,
            grid=(n_tok // tm,),
            in_specs=[pl.BlockSpec((tm, d), lambda i, pos: (i, 0)),
                      pl.BlockSpec(memory_space=pl.ANY)],
            out_specs=pl.BlockSpec(memory_space=pl.ANY),
            scratch_shapes=[pltpu.SemaphoreType.DMA(())],
        ),
        out_shape=jax.ShapeDtypeStruct((n_rows, d), F32),
        input_output_aliases={2: 0},
        compiler_params=_cparams("arbitrary"),
        name="moe_dispatch",
    )(pos, mx, xs0)


_ACTIVE, _FIRST = 1, 2


def _ffn_kernel(te_ref, tx_ref, fl_ref, xs_ref, wg_ref, wu_ref, wd_ref, ys_ref,
                wg_b, wu_b, wd_b):
    del te_ref, tx_ref
    flags = fl_ref[pl.program_id(0)]

    @pl.when((flags & _FIRST) != 0)
    def _():
        wg_b[...] = wg_ref[...].astype(BF16)
        wu_b[...] = wu_ref[...].astype(BF16)
        wd_b[...] = wd_ref[...].astype(BF16)

    @pl.when((flags & _ACTIVE) != 0)
    def _():
        x = xs_ref[...].astype(BF16)
        a = _dot(x, wg_b[...])
        b = _dot(x, wu_b[...])
        ys_ref[...] = _dot((jax.nn.silu(a) * b).astype(BF16), wd_b[...])

    @pl.when((flags & _ACTIVE) == 0)
    def _():
        ys_ref[...] = jnp.zeros_like(ys_ref)


def _expert_ffn(tile_e, tile_x, flags, xs, w_gate, w_up, w_down, tme):
    n_rows, d = xs.shape
    n_tiles = n_rows // tme
    f = w_gate.shape[-1]
    return pl.pallas_call(
        _ffn_kernel,
        grid_spec=pltpu.PrefetchScalarGridSpec(
            num_scalar_prefetch=3,
            grid=(n_tiles,),
            in_specs=[
                pl.BlockSpec((tme, d), lambda i, te, tx, fl: (tx[i], 0)),
                pl.BlockSpec((None, d, f), lambda i, te, tx, fl: (te[i], 0, 0)),
                pl.BlockSpec((None, d, f), lambda i, te, tx, fl: (te[i], 0, 0)),
                pl.BlockSpec((None, f, d), lambda i, te, tx, fl: (te[i], 0, 0)),
            ],
            out_specs=pl.BlockSpec((tme, d), lambda i, te, tx, fl: (i, 0)),
            scratch_shapes=[pltpu.VMEM((d, f), BF16), pltpu.VMEM((d, f), BF16),
                            pltpu.VMEM((f, d), BF16)],
        ),
        out_shape=jax.ShapeDtypeStruct((n_rows, d), F32),
        compiler_params=_cparams("arbitrary"),
        name="moe_expert_ffn",
    )(tile_e, tile_x, flags, xs, w_gate, w_up, w_down)


def _combine_kernel(pos_ref, ys_ref, h_ref, gt_ref, wf_ref, o_ref, buf, sems, *, n_tok, top_k):
    tm = h_ref.shape[0]
    i = pl.program_id(0)
    n = pl.num_programs(0)

    def gather(tile, slot, wait):
        def body(r, carry):
            for kk in range(top_k):
                src = 0 if wait else pos_ref[kk * n_tok + tile * tm + r]
                cp = pltpu.make_async_copy(ys_ref.at[pl.ds(src, 1), :],
                                           buf.at[slot, kk, pl.ds(r, 1), :], sems.at[slot])
                cp.wait() if wait else cp.start()
            return carry
        lax.fori_loop(0, tm, body, 0, unroll=8)

    @pl.when(i == 0)
    def _():
        gather(0, 0, False)

    @pl.when(i + 1 < n)
    def _():
        gather(i + 1, (i + 1) % 2, False)

    slot = i % 2
    gather(i, slot, True)
    wf = wf_ref[...]
    y = buf[slot, 0] * wf[:, 0:1]
    for kk in range(1, top_k):
        y = y + buf[slot, kk] * wf[:, kk:kk + 1]
    o_ref[...] = h_ref[...] + gt_ref[...] * y


def _combine(pos, ys, h, modl, wf, tiles, n_tiles, off, top_k):
    n_batch, _, d = h.shape
    tm = tiles.tm
    n_tok = n_batch * n_tiles * tm
    nb, ct = n_batch, tiles.ctx_tiles
    kern = functools.partial(_combine_kernel, n_tok=n_tok, top_k=top_k)

    def bt(i):
        return i // n_tiles, i % n_tiles

    return pl.pallas_call(
        kern,
        grid_spec=pltpu.PrefetchScalarGridSpec(
            num_scalar_prefetch=1,
            grid=(n_batch * n_tiles,),
            in_specs=[
                pl.BlockSpec(memory_space=pl.ANY),
                pl.BlockSpec((None, tm, d), lambda i, pos: (bt(i)[0], bt(i)[1] + off, 0)),
                pl.BlockSpec((None, None, 1, d),
                             lambda i, pos: (5, jnp.where(bt(i)[1] < ct, nb, bt(i)[0]), 0, 0)),
                pl.BlockSpec((tm, LANES), lambda i, pos: (i, 0)),
            ],
            out_specs=pl.BlockSpec((None, tm, d), lambda i, pos: (bt(i)[0], bt(i)[1], 0)),
            scratch_shapes=[pltpu.VMEM((2, top_k, tm, d), F32), pltpu.SemaphoreType.DMA((2,))],
        ),
        out_shape=jax.ShapeDtypeStruct((n_batch, n_tiles * tm, d), F32),
        compiler_params=_cparams("arbitrary"),
        name="moe_combine",
    )(pos, ys, h, modl, wf)


def _moe_layer(h, modl, g2, w_grp, b_grp, w_exp, b_exp, w_gate, w_up, w_down, tiles, n_tiles, off):
    n_batch, _, d = h.shape
    tm = tiles.tm
    n_groups, n_experts = w_gate.shape[0], w_gate.shape[1]
    n_all = n_groups * n_experts
    top_k = 2
    tme = MOE_ROW_TILE
    n_tok = n_batch * n_tiles * tm

    w_r = jnp.zeros((d, LANES), F32).at[:, :n_groups].set(w_grp).at[:, n_groups:n_groups + n_all].set(w_exp)
    b_r = jnp.zeros((1, LANES), F32).at[0, :n_groups].set(b_grp).at[0, n_groups:n_groups + n_all].set(b_exp)
    mx, ei, wf, cnt = _norm_router(h, modl, g2, w_r, b_r, tiles, n_tiles, off, n_groups, n_experts)

    counts = cnt[0, n_groups:n_groups + n_all]
    padded = ((counts + tme - 1) // tme) * tme
    ends = jnp.cumsum(padded)
    offs = ends - padded
    pos = (offs[ei[:, :top_k]] + ei[:, top_k:2 * top_k]).T.reshape(-1).astype(jnp.int32)
    n_tile_max = (top_k * n_tok + n_all * (tme - 1)) // tme + 1
    n_rows = n_tile_max * tme
    n_active = ends[-1] // tme
    last = jnp.maximum(n_active - 1, 0)
    tidx = jnp.arange(n_tile_max, dtype=jnp.int32)
    active = tidx < n_active
    tile_x = jnp.where(active, tidx, last).astype(jnp.int32)
    tile_e = jnp.minimum(jnp.searchsorted(ends, tile_x * tme, side="right"), n_all - 1).astype(jnp.int32)
    prev_e = jnp.concatenate([jnp.full((1,), -1, jnp.int32), tile_e[:-1]])
    first = jnp.logical_and(active, tile_e != prev_e)
    flags = (active.astype(jnp.int32) * _ACTIVE + first.astype(jnp.int32) * _FIRST).astype(jnp.int32)

    xs = _dispatch(pos, mx, n_rows, tm, top_k)
    ys = _expert_ffn(tile_e, tile_x, flags, xs,
                     w_gate.reshape(n_all, d, -1), w_up.reshape(n_all, d, -1),
                     w_down.reshape(n_all, -1, d), tme)
    return _combine(pos, ys, h, modl, wf, tiles, n_tiles, off, top_k)


def _rope_tables(n_ctx, n_lat, head_dim):
    rows = n_lat // GRID_W
    row_id = jnp.repeat(jnp.arange(rows, dtype=F32), GRID_W)
    col_id = jnp.tile(jnp.arange(GRID_W, dtype=F32), rows)
    half = head_dim // 2
    inv_freq = ROPE_THETA ** (-jnp.arange(0, half, 2, dtype=F32) / half)
    ang_r = row_id[:, None] * inv_freq[None, :]
    ang_c = col_id[:, None] * inv_freq[None, :]
    ang = jnp.concatenate([ang_r, ang_r, ang_c, ang_c], axis=-1)
    reps = HEAD_LANES // head_dim
    cos = jnp.tile(jnp.cos(ang), (1, reps))
    sin = jnp.tile(jnp.sin(ang), (1, reps))
    quarter = head_dim // 4
    first = (jnp.arange(HEAD_LANES) % (2 * quarter)) < quarter
    s_dn = jnp.where(first[None, :], -sin, 0.0)
    s_up = jnp.where(first[None, :], 0.0, sin)
    pad = lambda t, fill: jnp.concatenate([jnp.full((n_ctx, HEAD_LANES), fill, F32), t], axis=0)
    return pad(cos, 1.0), pad(s_dn, 0.0), pad(s_up, 0.0)


def kernel(x, c, ctx, c_ctx, ada_w, ada_b, norm1_g, norm2_g, gmlp_w_in, gmlp_v_gain, gmlp_w_s, gmlp_b_s, gmlp_w_out, gqa_w_qkv, gqa_q_gain, gqa_k_gain, gqa_w_o, diff_w_qkv, diff_q_gain, diff_k_gain, diff_lam_q1, diff_lam_k1, diff_lam_q2, diff_lam_k2, diff_sub_gain, diff_w_o, moe_w_grp, moe_b_grp, moe_w_exp, moe_b_exp, moe_w_gate, moe_w_up, moe_w_down):
    n_batch, n_lat, d = x.shape
    n_ctx = ctx.shape[1]
    depth = ada_w.shape[0]
    tm = 256 if (n_ctx % 256 == 0 and n_lat % 256 == 0) else 128
    assert n_ctx % tm == 0 and n_lat % tm == 0 and tm % CHUNK == 0 and d % LANES == 0
    ctx_tiles, lat_tiles = n_ctx // tm, n_lat // tm

    mod_rows = -(-(n_batch + 1) // 8) * 8
    cc = jnp.zeros((mod_rows, d), F32).at[:n_batch].set(c).at[n_batch].set(c_ctx)
    mod = _ada_table(cc, ada_w, ada_b)
    mod = mod.reshape(depth, mod_rows, 6, 1, d).transpose(0, 2, 1, 3, 4)

    gqa_heads = gqa_w_o.shape[1] // HEAD_LANES
    gqa_kv_heads = (gqa_w_qkv.shape[2] // HEAD_LANES - gqa_heads) // 2
    diff_heads = diff_w_o.shape[1] // HEAD_LANES
    rope_b = _rope_tables(n_ctx, n_lat, HEAD_LANES)
    rope_c = _rope_tables(n_ctx, n_lat, HEAD_LANES // 2)

    h = jnp.concatenate([ctx, x], axis=1)
    has_ctx = True
    for i in range(depth):
        kind, j = i % N_MIXERS, i // N_MIXERS
        need_ctx = any((k % N_MIXERS) != 0 for k in range(i + 1, depth))
        modl = mod[i]
        g1 = norm1_g[i].reshape(1, d)
        g2 = norm2_g[i].reshape(1, d)
        in_tiles = _Tiles(n_batch, tm, ctx_tiles if has_ctx else 0)
        n_in = (ctx_tiles if has_ctx else 0) + lat_tiles
        keep = has_ctx and need_ctx
        out_tiles = _Tiles(n_batch, tm, ctx_tiles if keep else 0)
        n_out = (ctx_tiles if keep else 0) + lat_tiles
        off = n_in - n_out

        if kind == 0:
            h = _gmlp_layer(h, modl, g1, gmlp_w_in[j].astype(BF16), gmlp_v_gain[j].reshape(1, -1),
                            gmlp_w_s[j].astype(BF16), gmlp_b_s[j].T, gmlp_w_out[j].astype(BF16),
                            out_tiles, n_out, off)
        else:
            if kind == 1:
                q, k, v = _qkv_proj(
                    h, modl, g1, gqa_w_qkv[j].astype(BF16), gqa_q_gain[j].reshape(1, -1),
                    gqa_k_gain[j].reshape(1, -1), rope_b if has_ctx else tuple(t[n_ctx:] for t in rope_b),
                    in_tiles, n_in, n_q=gqa_heads, n_k=gqa_kv_heads, n_v=gqa_kv_heads, halves=1,
                    quarter=HEAD_LANES // 4, q_scale=1.0 / math.sqrt(HEAD_LANES))
                o = _gqa_attention(q, k, v, tm, n_ctx if has_ctx else 0, n_out, off)
                w_o = gqa_w_o[j]
            else:
                lam_init = 0.8 - 0.6 * math.exp(-0.3 * i)
                q, k, v = _qkv_proj(
                    h, modl, g1, diff_w_qkv[j].astype(BF16), diff_q_gain[j].reshape(1, -1),
                    diff_k_gain[j].reshape(1, -1), rope_c if has_ctx else tuple(t[n_ctx:] for t in rope_c),
                    in_tiles, n_in, n_q=diff_heads, n_k=diff_heads, n_v=diff_heads, halves=2,
                    quarter=HEAD_LANES // 8, q_scale=1.0 / math.sqrt(HEAD_LANES // 2))
                lam_vecs = [t[j].reshape(1, -1) for t in (diff_lam_q1, diff_lam_k1, diff_lam_q2, diff_lam_k2)]
                o = _diff_attention(q, k, v, lam_vecs, diff_sub_gain[j].reshape(1, -1), lam_init, tm,
                                    n_ctx if has_ctx else 0, n_out, off)
                w_o = diff_w_o[j]
            h = _out_proj(o, w_o.astype(BF16), h, modl, out_tiles, n_out, off)

        h = _moe_layer(h, modl, g2, moe_w_grp[i], moe_b_grp[i], moe_w_exp[i], moe_b_exp[i],
                       moe_w_gate[i], moe_w_up[i], moe_w_down[i], out_tiles, n_out, 0)
        has_ctx = keep
    return h[:, n_ctx:] if has_ctx else h
```

```python
import functools
import math

import jax
import jax.numpy as jnp
from jax import lax
from jax.experimental import pallas as pl
from jax.experimental.pallas import tpu as pltpu

EPS = 1e-6
GRID_W = 64
ROPE_THETA = 10000.0
N_MIXERS = 3
CHUNK = 128
LANES = 128
HEAD_LANES = 128
MXU_COLS = 256
MOE_ROW_TILE = 128
VMEM_LIMIT_BYTES = 60000 * 1024
ADA_COL_TILE = 1024
LOG2E = math.log2(math.e)

F32 = jnp.float32
BF16 = jnp.bfloat16


def _cparams(*sem):
    return pltpu.CompilerParams(dimension_semantics=sem, vmem_limit_bytes=VMEM_LIMIT_BYTES)


def _resident(shape):
    nd = len(shape)
    return pl.BlockSpec(shape, lambda *_: (0,) * nd, pipeline_mode=pl.Buffered(1))


def _full(shape):
    nd = len(shape)
    return pl.BlockSpec(shape, lambda *_: (0,) * nd)


def _norm_mod(x, g, shift, scale):
    y = x * lax.rsqrt(jnp.mean(x * x, axis=-1, keepdims=True) + EPS)
    return (y * g) * (1.0 + scale) + shift


def _dot(a, b):
    return jnp.dot(a, b, preferred_element_type=F32)


def _ada_kernel(cc_ref, w_ref, b_ref, o_ref):
    s = jax.nn.silu(cc_ref[...])
    o_ref[...] = jnp.dot(s, w_ref[...], preferred_element_type=F32,
                         precision=lax.Precision.HIGHEST) + b_ref[...]


def _ada_table(cc, ada_w, ada_b):
    depth, d, n6 = ada_w.shape
    rows = cc.shape[0]
    tn = ADA_COL_TILE
    return pl.pallas_call(
        _ada_kernel,
        grid=(depth, n6 // tn),
        in_specs=[
            pl.BlockSpec((rows, d), lambda l, n: (0, 0)),
            pl.BlockSpec((None, d, tn), lambda l, n: (l, 0, n)),
            pl.BlockSpec((None, 1, tn), lambda l, n: (l, 0, n)),
        ],
        out_specs=pl.BlockSpec((None, rows, tn), lambda l, n: (l, 0, n)),
        out_shape=jax.ShapeDtypeStruct((depth, rows, n6), F32),
        compiler_params=_cparams("arbitrary", "arbitrary"),
        name="ada_table",
    )(cc, ada_w, ada_b.reshape(depth, 1, n6))


class _Tiles:
    def __init__(self, n_batch, tm, ctx_tiles):
        self.n_batch, self.tm, self.ctx_tiles = n_batch, tm, ctx_tiles

    def mod(self, j, d):
        nb, ct = self.n_batch, self.ctx_tiles
        return pl.BlockSpec((None, None, 1, d), lambda b, t: (j, jnp.where(t < ct, nb, b), 0, 0))

    def rows(self, d, off=0):
        tm = self.tm
        return pl.BlockSpec((None, tm, d), lambda b, t: (b, t + off, 0))


def _gmlp_kernel(h_ref, sh_ref, sc_ref, gt_ref, g_ref, win_ref, vg_ref, ws_ref, bs_ref, wout_ref,
                 o_ref, gated_ref, *, width, groups):
    x = h_ref[...]
    tm = x.shape[0]
    a = _norm_mod(x, g_ref[...], sh_ref[...], sc_ref[...]).astype(BF16)
    v = jax.nn.gelu(_dot(a, win_ref[:, width:]))
    v = v * lax.rsqrt(jnp.mean(v * v, axis=-1, keepdims=True) + EPS) * vg_ref[...]
    v = v.astype(BF16)
    gd = width // groups
    per = MXU_COLS // gd
    for g0 in range(0, groups, per):
        u = jax.nn.gelu(_dot(a, win_ref[:, g0 * gd:(g0 + per) * gd]))
        for g in range(g0, g0 + per):
            cols = slice(g * gd, (g + 1) * gd)
            ug = u[:, (g - g0) * gd:(g - g0 + 1) * gd]
            for c in range(tm // CHUNK):
                rows = slice(c * CHUNK, (c + 1) * CHUNK)
                s = _dot(ws_ref[g], v[rows, cols]) + bs_ref[:, g:g + 1]
                gated_ref[rows, cols] = (ug[rows] * s).astype(BF16)
    y = _dot(gated_ref[...], wout_ref[...])
    o_ref[...] = x + gt_ref[...] * y


def _gmlp_layer(h, modl, g1, w_in, v_gain, w_s, b_s, w_out, tiles, n_tiles, off):
    n_batch, _, d = h.shape
    tm = tiles.tm
    width = w_out.shape[0]
    groups = w_s.shape[0]
    kern = functools.partial(_gmlp_kernel, width=width, groups=groups)
    return pl.pallas_call(
        kern,
        grid=(n_batch, n_tiles),
        in_specs=[
            tiles.rows(d, off), tiles.mod(0, d), tiles.mod(1, d), tiles.mod(2, d),
            _full((1, d)), _resident(w_in.shape), _full((1, width)), _resident(w_s.shape),
            _full(b_s.shape), _resident(w_out.shape),
        ],
        out_specs=tiles.rows(d),
        out_shape=jax.ShapeDtypeStruct((n_batch, n_tiles * tm, d), F32),
        scratch_shapes=[pltpu.VMEM((tm, width), BF16)],
        compiler_params=_cparams("arbitrary", "arbitrary"),
        name="gmlp_mixer",
    )(h, modl, modl, modl, g1, w_in, v_gain, w_s, b_s, w_out)


def _rope(x, cos, s_dn, s_up, quarter):
    return (x * cos + pltpu.roll(x, LANES - quarter, 1) * s_dn + pltpu.roll(x, quarter, 1) * s_up)


def _qkv_kernel(h_ref, sh_ref, sc_ref, g_ref, w_ref, qg_ref, kg_ref, cos_ref, sdn_ref, sup_ref,
                q_ref, k_ref, v_ref, *, n_q, n_k, n_v, halves, quarter, q_scale):
    x = h_ref[...]
    a = _norm_mod(x, g_ref[...], sh_ref[...], sc_ref[...]).astype(BF16)
    cos, sdn, sup = cos_ref[...], sdn_ref[...], sup_ref[...]
    lane = lax.broadcasted_iota(jnp.int32, (1, HEAD_LANES), 1)
    lo = lane < (HEAD_LANES // 2)

    def head_norm(t, gain):
        sq = t * t
        if halves == 1:
            inv = lax.rsqrt(jnp.mean(sq, axis=-1, keepdims=True) + EPS)
        else:
            tot = jnp.sum(sq, axis=-1, keepdims=True)
            s_lo = jnp.sum(jnp.where(lo, sq, 0.0), axis=-1, keepdims=True)
            half = HEAD_LANES // 2
            inv = jnp.where(lo, lax.rsqrt(s_lo / half + EPS), lax.rsqrt((tot - s_lo) / half + EPS))
        return t * inv * gain

    qg, kg = qg_ref[...], kg_ref[...]
    per = MXU_COLS // HEAD_LANES
    for j0 in range(0, n_q + n_k + n_v, per):
        t2 = _dot(a, w_ref[:, j0 * HEAD_LANES:(j0 + per) * HEAD_LANES])
        for j in range(j0, j0 + per):
            t = t2[:, (j - j0) * HEAD_LANES:(j - j0 + 1) * HEAD_LANES]
            if j < n_q:
                t = _rope(head_norm(t, qg), cos, sdn, sup, quarter) * q_scale
                q_ref[j] = t.astype(BF16)
            elif j < n_q + n_k:
                k_ref[j - n_q] = _rope(head_norm(t, kg), cos, sdn, sup, quarter).astype(BF16)
            else:
                v_ref[j - n_q - n_k] = t.astype(BF16)


def _qkv_proj(h, modl, g1, w, q_gain, k_gain, rope_tabs, tiles, n_tiles, *, n_q, n_k, n_v, halves,
              quarter, q_scale):
    n_batch, n_rows, d = h.shape
    tm = tiles.tm
    cos, sdn, sup = rope_tabs
    kern = functools.partial(_qkv_kernel, n_q=n_q, n_k=n_k, n_v=n_v, halves=halves,
                             quarter=quarter, q_scale=q_scale)
    tab = pl.BlockSpec((tm, HEAD_LANES), lambda b, t: (t, 0))

    def heads(n):
        return pl.BlockSpec((None, n, tm, HEAD_LANES), lambda b, t: (b, 0, t, 0))

    def out(n):
        return jax.ShapeDtypeStruct((n_batch, n, n_rows, HEAD_LANES), BF16)

    return pl.pallas_call(
        kern,
        grid=(n_batch, n_tiles),
        in_specs=[
            tiles.rows(d), tiles.mod(0, d), tiles.mod(1, d), _full((1, d)), _resident(w.shape),
            _full((1, HEAD_LANES)), _full((1, HEAD_LANES)), tab, tab, tab,
        ],
        out_specs=[heads(n_q), heads(n_k), heads(n_v)],
        out_shape=[out(n_q), out(n_k), out(n_v)],
        compiler_params=_cparams("arbitrary", "arbitrary"),
        name="qkv_proj",
    )(h, modl, modl, g1, w, q_gain, k_gain, cos, sdn, sup)


def _softmax_pv(q, k, v):
    s = lax.dot_general(q, k, (((1,), (1,)), ((), ())), preferred_element_type=F32)
    p = jnp.exp2(s - jnp.max(s, axis=-1, keepdims=True))
    o = _dot(p.astype(BF16), v)
    return o / jnp.sum(p, axis=-1, keepdims=True)


def _gqa_attn_kernel(q_ref, k_ref, v_ref, o_ref, *, group, n_ctx, ctx_tiles):
    qi = pl.program_id(2)

    def attend(n_keys):
        k, v = k_ref[:n_keys], v_ref[:n_keys]
        for g in range(group):
            o = _softmax_pv(q_ref[g], k, v)
            o_ref[:, g * HEAD_LANES:(g + 1) * HEAD_LANES] = o.astype(BF16)

    if ctx_tiles:
        pl.when(qi < ctx_tiles)(lambda: attend(n_ctx))
        pl.when(qi >= ctx_tiles)(lambda: attend(k_ref.shape[0]))
    else:
        attend(k_ref.shape[0])


def _gqa_attention(q, k, v, tq, n_ctx, q_tiles, q_off):
    n_batch, n_heads, n_rows, _ = q.shape
    kv_heads = k.shape[1]
    group = n_heads // kv_heads
    ctx_tiles = 0 if q_off else n_ctx // tq
    kern = functools.partial(_gqa_attn_kernel, group=group, n_ctx=n_ctx, ctx_tiles=ctx_tiles)
    kv = pl.BlockSpec((None, None, n_rows, HEAD_LANES), lambda b, h, t: (b, h, 0, 0))
    return pl.pallas_call(
        kern,
        grid=(n_batch, kv_heads, q_tiles),
        in_specs=[
            pl.BlockSpec((None, group, tq, HEAD_LANES), lambda b, h, t: (b, h, t + q_off, 0)),
            kv, kv,
        ],
        out_specs=pl.BlockSpec((None, tq, group * HEAD_LANES), lambda b, h, t: (b, t, h)),
        out_shape=jax.ShapeDtypeStruct((n_batch, q_tiles * tq, n_heads * HEAD_LANES), BF16),
        compiler_params=_cparams("arbitrary", "arbitrary", "arbitrary"),
        name="gqa_attention",
    )(q, k, v)


def _diff_attn_kernel(q_ref, k_ref, v_ref, lq1_ref, lk1_ref, lq2_ref, lk2_ref, sg_ref, o_ref, *,
                      lam_init, n_ctx, ctx_tiles):
    qi = pl.program_id(2)
    lam = (jnp.exp(jnp.sum(lq1_ref[...] * lk1_ref[...], axis=-1, keepdims=True))
           - jnp.exp(jnp.sum(lq2_ref[...] * lk2_ref[...], axis=-1, keepdims=True)) + lam_init)
    lane = lax.broadcasted_iota(jnp.int32, (1, HEAD_LANES), 1)
    lo = lane < (HEAD_LANES // 2)

    def attend(n_keys):
        q, k, v = q_ref[...], k_ref[:n_keys], v_ref[:n_keys]
        zero = jnp.zeros_like(q)
        dims = (((1,), (1,)), ((), ()))
        s0 = lax.dot_general(jnp.where(lo, q, zero), k, dims, preferred_element_type=F32)
        s1 = lax.dot_general(jnp.where(lo, zero, q), k, dims, preferred_element_type=F32)
        e0 = jnp.exp2(s0 - jnp.max(s0, axis=-1, keepdims=True))
        e1 = jnp.exp2(s1 - jnp.max(s1, axis=-1, keepdims=True))
        c0 = 1.0 / jnp.sum(e0, axis=-1, keepdims=True)
        c1 = lam / jnp.sum(e1, axis=-1, keepdims=True)
        o = _dot(e0.astype(BF16), v) * c0 - _dot(e1.astype(BF16), v) * c1
        o = o * lax.rsqrt(jnp.mean(o * o, axis=-1, keepdims=True) + EPS) * sg_ref[...]
        o_ref[...] = (o * (1.0 - lam_init)).astype(BF16)

    if ctx_tiles:
        pl.when(qi < ctx_tiles)(lambda: attend(n_ctx))
        pl.when(qi >= ctx_tiles)(lambda: attend(k_ref.shape[0]))
    else:
        attend(k_ref.shape[0])


def _diff_attention(q, k, v, lam_vecs, sub_gain, lam_init, tq, n_ctx, q_tiles, q_off):
    n_batch, n_heads, n_rows, _ = q.shape
    ctx_tiles = 0 if q_off else n_ctx // tq
    kern = functools.partial(_diff_attn_kernel, lam_init=lam_init, n_ctx=n_ctx, ctx_tiles=ctx_tiles)
    kv = pl.BlockSpec((None, None, n_rows, HEAD_LANES), lambda b, h, t: (b, h, 0, 0))
    vec = _full(lam_vecs[0].shape)
    return pl.pallas_call(
        kern,
        grid=(n_batch, n_heads, q_tiles),
        in_specs=[
            pl.BlockSpec((None, None, tq, HEAD_LANES), lambda b, h, t: (b, h, t + q_off, 0)),
            kv, kv, vec, vec, vec, vec, _full((1, HEAD_LANES)),
        ],
        out_specs=pl.BlockSpec((None, tq, HEAD_LANES), lambda b, h, t: (b, t, h)),
        out_shape=jax.ShapeDtypeStruct((n_batch, q_tiles * tq, n_heads * HEAD_LANES), BF16),
        compiler_params=_cparams("arbitrary", "arbitrary", "arbitrary"),
        name="diff_attention",
    )(q, k, v, *lam_vecs, sub_gain)


def _oproj_kernel(o_ref, w_ref, h_ref, gt_ref, out_ref):
    out_ref[...] = h_ref[...] + gt_ref[...] * _dot(o_ref[...], w_ref[...])


def _out_proj(o, w, h, modl, tiles, n_tiles, h_off):
    n_batch, _, d = h.shape
    tm = tiles.tm
    return pl.pallas_call(
        _oproj_kernel,
        grid=(n_batch, n_tiles),
        in_specs=[tiles.rows(o.shape[-1]), _resident(w.shape), tiles.rows(d, h_off), tiles.mod(2, d)],
        out_specs=tiles.rows(d),
        out_shape=jax.ShapeDtypeStruct((n_batch, n_tiles * tm, d), F32),
        compiler_params=_cparams("arbitrary", "arbitrary"),
        name="out_proj",
    )(o, w, h, modl)


def _router_kernel(h_ref, sh_ref, sc_ref, g_ref, wr_ref, br_ref, mx_ref, ei_ref, wf_ref, cnt_ref,
                   carry_ref, *, n_groups, n_experts):
    first = jnp.logical_and(pl.program_id(0) == 0, pl.program_id(1) == 0)

    @pl.when(first)
    def _():
        carry_ref[...] = jnp.zeros_like(carry_ref)

    x = h_ref[...]
    tm = x.shape[0]
    mx = _norm_mod(x, g_ref[...], sh_ref[...], sc_ref[...])
    mx_ref[...] = mx
    logits = jnp.dot(mx, wr_ref[...], preferred_element_type=F32,
                     precision=lax.Precision.HIGHEST) + br_ref[...]
    lane = lax.broadcasted_iota(jnp.int32, logits.shape, 1).astype(F32)
    neg = jnp.float32(-jnp.inf)
    big = jnp.float32(LANES)

    def first_argmax(vals):
        m = jnp.max(vals, axis=-1, keepdims=True)
        return m, jnp.min(jnp.where(vals == m, lane, big), axis=-1, keepdims=True)

    glog = jnp.where(lane < n_groups, logits, neg)
    gmax, gidx = first_argmax(glog)
    grp_w = 1.0 / jnp.sum(jnp.exp(glog - gmax), axis=-1, keepdims=True)
    e_lo = n_groups + gidx * n_experts
    elog = jnp.where(jnp.logical_and(lane >= e_lo, lane < e_lo + n_experts), logits, neg)
    v1, i1 = first_argmax(elog)
    v2, i2 = first_argmax(jnp.where(lane == i1, neg, elog))
    t = jnp.exp(v2 - v1)
    w1 = grp_w / (1.0 + t)
    w2 = grp_w * t / (1.0 + t)

    onehot = jnp.logical_or(lane == i1, lane == i2)
    r_io = lax.broadcasted_iota(jnp.int32, (tm, tm), 0)
    c_io = lax.broadcasted_iota(jnp.int32, (tm, tm), 1)
    tri = jnp.where(r_io >= c_io, 1.0, 0.0).astype(BF16)
    incl = _dot(tri, jnp.where(onehot, 1.0, 0.0).astype(BF16))
    before = carry_ref[...] + incl - 1.0
    r1 = jnp.sum(jnp.where(lane == i1, before, 0.0), axis=-1, keepdims=True)
    r2 = jnp.sum(jnp.where(lane == i2, before, 0.0), axis=-1, keepdims=True)
    carry = carry_ref[...] + incl[tm - 1:tm, :]
    carry_ref[...] = carry
    cnt_ref[...] = carry.astype(jnp.int32)

    ei = jnp.where(lane == 0, i1 - n_groups,
                   jnp.where(lane == 1, i2 - n_groups,
                             jnp.where(lane == 2, r1, jnp.where(lane == 3, r2, 0.0))))
    ei_ref[...] = ei.astype(jnp.int32)
    wf_ref[...] = jnp.where(lane == 0, w1, jnp.where(lane == 1, w2, 0.0))


def _norm_router(h, modl, g2, w_r, b_r, tiles, n_tiles, off, n_groups, n_experts):
    n_batch, _, d = h.shape
    tm = tiles.tm
    n_tok = n_batch * n_tiles * tm
    kern = functools.partial(_router_kernel, n_groups=n_groups, n_experts=n_experts)

    def flat(width):
        return pl.BlockSpec((tm, width), lambda b, t: (b * n_tiles + t, 0))

    return pl.pallas_call(
        kern,
        grid=(n_batch, n_tiles),
        in_specs=[tiles.rows(d, off), tiles.mod(3, d), tiles.mod(4, d), _full((1, d)),
                  _full(w_r.shape), _full((1, LANES))],
        out_specs=[flat(d), flat(LANES), flat(LANES), _full((1, LANES))],
        out_shape=[
            jax.ShapeDtypeStruct((n_tok, d), F32),
            jax.ShapeDtypeStruct((n_tok, LANES), jnp.int32),
            jax.ShapeDtypeStruct((n_tok, LANES), F32),
            jax.ShapeDtypeStruct((1, LANES), jnp.int32),
        ],
        scratch_shapes=[pltpu.VMEM((1, LANES), F32)],
        compiler_params=_cparams("arbitrary", "arbitrary"),
        name="moe_norm_router",
    )(h, modl, modl, g2, w_r, b_r)


def _row_copy(src_ref, src_row, dst_ref, dst_row, sem):
    return pltpu.make_async_copy(src_ref.at[pl.ds(src_row, 1), :], dst_ref.at[pl.ds(dst_row, 1), :], sem)


def _dispatch_kernel(pos_ref, mx_ref, xs_in_ref, xs_ref, sem, *, n_tok, top_k):
    del xs_in_ref
    tm = mx_ref.shape[0]
    base = pl.program_id(0) * tm

    def start(r, carry):
        for kk in range(top_k):
            _row_copy(mx_ref, r, xs_ref, pos_ref[kk * n_tok + base + r], sem).start()
        return carry

    lax.fori_loop(0, tm, start, 0, unroll=8)

    def wait(r, carry):
        for kk in range(top_k):
            _row_copy(mx_ref, 0, xs_ref, 0, sem).wait()
        return carry

    lax.fori_loop(0, tm, wait, 0, unroll=8)


def _dispatch(pos, mx, n_rows, tm, top_k):
    n_tok, d = mx.shape
    kern = functools.partial(_dispatch_kernel, n_tok=n_tok, top_k=top_k)
    xs0 = jnp.zeros((n_rows, d), F32)
    return pl.pallas_call(
        kern,
        grid_spec=pltpu.PrefetchScalarGridSpec(
            num_scalar_prefetch=1,
            grid=(n_tok // tm,),
            in_specs=[pl.BlockSpec((tm, d), lambda i, pos: (i, 0)),
                      pl.BlockSpec(memory_space=pl.ANY)],
            out_specs=pl.BlockSpec(memory_space=pl.ANY),
            scratch_shapes=[pltpu.SemaphoreType.DMA(())],
        ),
        out_shape=jax.ShapeDtypeStruct((n_rows, d), F32),
        input_output_aliases={2: 0},
        compiler_params=_cparams("arbitrary"),
        name="moe_dispatch",
    )(pos, mx, xs0)


_ACTIVE, _FIRST = 1, 2


def _ffn_kernel(te_ref, tn_ref, tx_ref, fl_ref, xs_ref, wg_hbm, wu_hbm, wd_hbm, ys_ref,
                wg_f, wu_f, wd_f, wg_b, wu_b, wd_b, sems):
    del tx_ref
    i = pl.program_id(0)
    flags = fl_ref[i]

    def weight_copies(e):
        return (pltpu.make_async_copy(wg_hbm.at[e], wg_f, sems.at[0]),
                pltpu.make_async_copy(wu_hbm.at[e], wu_f, sems.at[1]),
                pltpu.make_async_copy(wd_hbm.at[e], wd_f, sems.at[2]))

    @pl.when(i == 0)
    def _():
        for cp in weight_copies(te_ref[0]):
            cp.start()

    @pl.when((flags & _FIRST) != 0)
    def _():
        for cp in weight_copies(te_ref[i]):
            cp.wait()
        wg_b[...] = wg_f[...].astype(BF16)
        wu_b[...] = wu_f[...].astype(BF16)
        wd_b[...] = wd_f[...].astype(BF16)

        @pl.when(tn_ref[i] >= 0)
        def _():
            for cp in weight_copies(tn_ref[i]):
                cp.start()

    @pl.when((flags & _ACTIVE) != 0)
    def _():
        x = xs_ref[...].astype(BF16)
        a = _dot(x, wg_b[...])
        b = _dot(x, wu_b[...])
        ys_ref[...] = _dot((jax.nn.silu(a) * b).astype(BF16), wd_b[...])

    @pl.when((flags & _ACTIVE) == 0)
    def _():
        ys_ref[...] = jnp.zeros_like(ys_ref)


def _expert_ffn(tile_e, tile_next, tile_x, flags, xs, w_gate, w_up, w_down, tme):
    n_rows, d = xs.shape
    n_tiles = n_rows // tme
    f = w_gate.shape[-1]
    any_spec = pl.BlockSpec(memory_space=pl.ANY)
    return pl.pallas_call(
        _ffn_kernel,
        grid_spec=pltpu.PrefetchScalarGridSpec(
            num_scalar_prefetch=4,
            grid=(n_tiles,),
            in_specs=[pl.BlockSpec((tme, d), lambda i, te, tn, tx, fl: (tx[i], 0)),
                      any_spec, any_spec, any_spec],
            out_specs=pl.BlockSpec((tme, d), lambda i, te, tn, tx, fl: (i, 0)),
            scratch_shapes=[pltpu.VMEM((d, f), F32), pltpu.VMEM((d, f), F32), pltpu.VMEM((f, d), F32),
                            pltpu.VMEM((d, f), BF16), pltpu.VMEM((d, f), BF16),
                            pltpu.VMEM((f, d), BF16), pltpu.SemaphoreType.DMA((3,))],
        ),
        out_shape=jax.ShapeDtypeStruct((n_rows, d), F32),
        compiler_params=_cparams("arbitrary"),
        name="moe_expert_ffn",
    )(tile_e, tile_next, tile_x, flags, xs, w_gate, w_up, w_down)


def _combine_kernel(pos_ref, ys_ref, h_ref, gt_ref, wf_ref, o_ref, buf, sems, *, n_tok, top_k):
    tm = h_ref.shape[0]
    i = pl.program_id(0)
    n = pl.num_programs(0)

    def gather(tile, slot, wait):
        def body(r, carry):
            for kk in range(top_k):
                src = 0 if wait else pos_ref[kk * n_tok + tile * tm + r]
                cp = pltpu.make_async_copy(ys_ref.at[pl.ds(src, 1), :],
                                           buf.at[slot, kk, pl.ds(r, 1), :], sems.at[slot])
                cp.wait() if wait else cp.start()
            return carry
        lax.fori_loop(0, tm, body, 0, unroll=8)

    @pl.when(i == 0)
    def _():
        gather(0, 0, False)

    @pl.when(i + 1 < n)
    def _():
        gather(i + 1, (i + 1) % 2, False)

    slot = i % 2
    gather(i, slot, True)
    wf = wf_ref[...]
    y = buf[slot, 0] * wf[:, 0:1]
    for kk in range(1, top_k):
        y = y + buf[slot, kk] * wf[:, kk:kk + 1]
    o_ref[...] = h_ref[...] + gt_ref[...] * y


def _combine(pos, ys, h, modl, wf, tiles, n_tiles, off, top_k):
    n_batch, _, d = h.shape
    tm = tiles.tm
    n_tok = n_batch * n_tiles * tm
    nb, ct = n_batch, tiles.ctx_tiles
    kern = functools.partial(_combine_kernel, n_tok=n_tok, top_k=top_k)

    def bt(i):
        return i // n_tiles, i % n_tiles

    return pl.pallas_call(
        kern,
        grid_spec=pltpu.PrefetchScalarGridSpec(
            num_scalar_prefetch=1,
            grid=(n_batch * n_tiles,),
            in_specs=[
                pl.BlockSpec(memory_space=pl.ANY),
                pl.BlockSpec((None, tm, d), lambda i, pos: (bt(i)[0], bt(i)[1] + off, 0)),
                pl.BlockSpec((None, None, 1, d),
                             lambda i, pos: (5, jnp.where(bt(i)[1] < ct, nb, bt(i)[0]), 0, 0)),
                pl.BlockSpec((tm, LANES), lambda i, pos: (i, 0)),
            ],
            out_specs=pl.BlockSpec((None, tm, d), lambda i, pos: (bt(i)[0], bt(i)[1], 0)),
            scratch_shapes=[pltpu.VMEM((2, top_k, tm, d), F32), pltpu.SemaphoreType.DMA((2,))],
        ),
        out_shape=jax.ShapeDtypeStruct((n_batch, n_tiles * tm, d), F32),
        compiler_params=_cparams("arbitrary"),
        name="moe_combine",
    )(pos, ys, h, modl, wf)


def _moe_layer(layer, h, modl, g2, w_grp, b_grp, w_exp, b_exp, w_gate, w_up, w_down, tiles, n_tiles, off):
    n_batch, _, d = h.shape
    tm = tiles.tm
    n_groups, n_experts = w_gate.shape[1], w_gate.shape[2]
    n_all = n_groups * n_experts
    top_k = 2
    tme = MOE_ROW_TILE
    n_tok = n_batch * n_tiles * tm

    w_r = jnp.zeros((d, LANES), F32).at[:, :n_groups].set(w_grp).at[:, n_groups:n_groups + n_all].set(w_exp)
    b_r = jnp.zeros((1, LANES), F32).at[0, :n_groups].set(b_grp).at[0, n_groups:n_groups + n_all].set(b_exp)
    mx, ei, wf, cnt = _norm_router(h, modl, g2, w_r, b_r, tiles, n_tiles, off, n_groups, n_experts)

    counts = cnt[0, n_groups:n_groups + n_all]
    padded = ((counts + tme - 1) // tme) * tme
    ends = jnp.cumsum(padded)
    offs = ends - padded
    eid = jnp.arange(n_all, dtype=jnp.int32)
    row0 = jnp.sum(jnp.where(ei[:, :top_k, None] == eid, offs.astype(jnp.int32), 0), axis=-1)
    pos = (row0 + ei[:, top_k:2 * top_k]).T.reshape(-1).astype(jnp.int32)
    n_tile_max = (top_k * n_tok + n_all * (tme - 1)) // tme + 1
    n_rows = n_tile_max * tme
    n_active = ends[-1] // tme
    last = jnp.maximum(n_active - 1, 0)
    tidx = jnp.arange(n_tile_max, dtype=jnp.int32)
    active = tidx < n_active
    tile_x = jnp.where(active, tidx, last).astype(jnp.int32)
    tile_e = jnp.sum((ends[None, :] <= (tile_x * tme)[:, None]).astype(jnp.int32), axis=1)
    tile_e = jnp.minimum(tile_e, n_all - 1)
    prev_e = jnp.concatenate([jnp.full((1,), -1, jnp.int32), tile_e[:-1]])
    first = jnp.logical_and(active, tile_e != prev_e)
    flags = (active.astype(jnp.int32) * _ACTIVE + first.astype(jnp.int32) * _FIRST).astype(jnp.int32)
    later = jnp.logical_and(eid[None, :] > eid[:, None], (padded > 0)[None, :])
    next_e = jnp.min(jnp.where(later, eid[None, :], n_all), axis=1)
    next_e = jnp.where(next_e < n_all, next_e + layer * n_all, -1).astype(jnp.int32)
    tile_next = jnp.sum(jnp.where(tile_e[:, None] == eid[None, :], next_e[None, :], 0), axis=1)
    tile_w = (tile_e + layer * n_all).astype(jnp.int32)

    xs = _dispatch(pos, mx, n_rows, tm, top_k)
    ys = _expert_ffn(tile_w, tile_next.astype(jnp.int32), tile_x, flags, xs,
                     w_gate.reshape(-1, d, w_gate.shape[-1]), w_up.reshape(-1, d, w_up.shape[-1]),
                     w_down.reshape(-1, w_down.shape[-2], d), tme)
    return _combine(pos, ys, h, modl, wf, tiles, n_tiles, off, top_k)


def _rope_tables(n_ctx, n_lat, head_dim):
    rows = n_lat // GRID_W
    row_id = jnp.repeat(jnp.arange(rows, dtype=F32), GRID_W)
    col_id = jnp.tile(jnp.arange(GRID_W, dtype=F32), rows)
    half = head_dim // 2
    inv_freq = ROPE_THETA ** (-jnp.arange(0, half, 2, dtype=F32) / half)
    ang_r = row_id[:, None] * inv_freq[None, :]
    ang_c = col_id[:, None] * inv_freq[None, :]
    ang = jnp.concatenate([ang_r, ang_r, ang_c, ang_c], axis=-1)
    reps = HEAD_LANES // head_dim
    cos = jnp.tile(jnp.cos(ang), (1, reps))
    sin = jnp.tile(jnp.sin(ang), (1, reps))
    quarter = head_dim // 4
    first = (jnp.arange(HEAD_LANES) % (2 * quarter)) < quarter
    s_dn = jnp.where(first[None, :], -sin, 0.0)
    s_up = jnp.where(first[None, :], 0.0, sin)
    pad = lambda t, fill: jnp.concatenate([jnp.full((n_ctx, HEAD_LANES), fill, F32), t], axis=0)
    return pad(cos, 1.0), pad(s_dn, 0.0), pad(s_up, 0.0)


def kernel(x, c, ctx, c_ctx, ada_w, ada_b, norm1_g, norm2_g, gmlp_w_in, gmlp_v_gain, gmlp_w_s, gmlp_b_s, gmlp_w_out, gqa_w_qkv, gqa_q_gain, gqa_k_gain, gqa_w_o, diff_w_qkv, diff_q_gain, diff_k_gain, diff_lam_q1, diff_lam_k1, diff_lam_q2, diff_lam_k2, diff_sub_gain, diff_w_o, moe_w_grp, moe_b_grp, moe_w_exp, moe_b_exp, moe_w_gate, moe_w_up, moe_w_down):
    n_batch, n_lat, d = x.shape
    n_ctx = ctx.shape[1]
    depth = ada_w.shape[0]
    tm = 256 if (n_ctx % 256 == 0 and n_lat % 256 == 0) else 128
    assert n_ctx % tm == 0 and n_lat % tm == 0 and tm % CHUNK == 0 and d % LANES == 0
    ctx_tiles, lat_tiles = n_ctx // tm, n_lat // tm

    mod_rows = -(-(n_batch + 1) // 8) * 8
    cc = jnp.zeros((mod_rows, d), F32).at[:n_batch].set(c).at[n_batch].set(c_ctx)
    mod = _ada_table(cc, ada_w, ada_b)
    mod = mod.reshape(depth, mod_rows, 6, 1, d).transpose(0, 2, 1, 3, 4)

    gqa_heads = gqa_w_o.shape[1] // HEAD_LANES
    gqa_kv_heads = (gqa_w_qkv.shape[2] // HEAD_LANES - gqa_heads) // 2
    diff_heads = diff_w_o.shape[1] // HEAD_LANES
    rope_b = _rope_tables(n_ctx, n_lat, HEAD_LANES)
    rope_c = _rope_tables(n_ctx, n_lat, HEAD_LANES // 2)

    h = jnp.concatenate([ctx, x], axis=1)
    has_ctx = True
    for i in range(depth):
        kind, j = i % N_MIXERS, i // N_MIXERS
        need_ctx = any((k % N_MIXERS) != 0 for k in range(i + 1, depth))
        modl = mod[i]
        g1 = norm1_g[i].reshape(1, d)
        g2 = norm2_g[i].reshape(1, d)
        in_tiles = _Tiles(n_batch, tm, ctx_tiles if has_ctx else 0)
        n_in = (ctx_tiles if has_ctx else 0) + lat_tiles
        keep = has_ctx and need_ctx
        out_tiles = _Tiles(n_batch, tm, ctx_tiles if keep else 0)
        n_out = (ctx_tiles if keep else 0) + lat_tiles
        off = n_in - n_out

        if kind == 0:
            h = _gmlp_layer(h, modl, g1, gmlp_w_in[j].astype(BF16), gmlp_v_gain[j].reshape(1, -1),
                            gmlp_w_s[j].astype(BF16), gmlp_b_s[j].T, gmlp_w_out[j].astype(BF16),
                            out_tiles, n_out, off)
        else:
            if kind == 1:
                q, k, v = _qkv_proj(
                    h, modl, g1, gqa_w_qkv[j].astype(BF16), gqa_q_gain[j].reshape(1, -1),
                    gqa_k_gain[j].reshape(1, -1), rope_b if has_ctx else tuple(t[n_ctx:] for t in rope_b),
                    in_tiles, n_in, n_q=gqa_heads, n_k=gqa_kv_heads, n_v=gqa_kv_heads, halves=1,
                    quarter=HEAD_LANES // 4, q_scale=LOG2E / math.sqrt(HEAD_LANES))
                o = _gqa_attention(q, k, v, tm, n_ctx if has_ctx else 0, n_out, off)
                w_o = gqa_w_o[j]
            else:
                lam_init = 0.8 - 0.6 * math.exp(-0.3 * i)
                q, k, v = _qkv_proj(
                    h, modl, g1, diff_w_qkv[j].astype(BF16), diff_q_gain[j].reshape(1, -1),
                    diff_k_gain[j].reshape(1, -1), rope_c if has_ctx else tuple(t[n_ctx:] for t in rope_c),
                    in_tiles, n_in, n_q=diff_heads, n_k=diff_heads, n_v=diff_heads, halves=2,
                    quarter=HEAD_LANES // 8, q_scale=LOG2E / math.sqrt(HEAD_LANES // 2))
                lam_vecs = [t[j].reshape(1, -1) for t in (diff_lam_q1, diff_lam_k1, diff_lam_q2, diff_lam_k2)]
                o = _diff_attention(q, k, v, lam_vecs, diff_sub_gain[j].reshape(1, -1), lam_init, tm,
                                    n_ctx if has_ctx else 0, n_out, off)
                w_o = diff_w_o[j]
            h = _out_proj(o, w_o.astype(BF16), h, modl, out_tiles, n_out, off)

        h = _moe_layer(i, h, modl, g2, moe_w_grp[i], moe_b_grp[i], moe_w_exp[i], moe_b_exp[i],
                       moe_w_gate, moe_w_up, moe_w_down, out_tiles, n_out, 0)
        has_ctx = keep
    return h[:, n_ctx:] if has_ctx else h
```

```python
import functools
import math

import jax
import jax.numpy as jnp
from jax import lax
from jax.experimental import pallas as pl
from jax.experimental.pallas import tpu as pltpu

EPS = 1e-6
GRID_W = 64
ROPE_THETA = 10000.0
N_MIXERS = 3
CHUNK = 128
LANES = 128
HEAD_LANES = 128
MXU_COLS = 256
MOE_ROW_TILE = 128
DIFF_HEADS_PER_STEP = 4
GQA_KV_HEADS_PER_STEP = 2
VMEM_LIMIT_BYTES = 60000 * 1024
ADA_COL_TILE = 1024
LOG2E = math.log2(math.e)

F32 = jnp.float32
BF16 = jnp.bfloat16


def _cparams(*sem):
    return pltpu.CompilerParams(dimension_semantics=sem, vmem_limit_bytes=VMEM_LIMIT_BYTES)


def _resident(shape):
    nd = len(shape)
    return pl.BlockSpec(shape, lambda *_: (0,) * nd, pipeline_mode=pl.Buffered(1))


def _full(shape):
    nd = len(shape)
    return pl.BlockSpec(shape, lambda *_: (0,) * nd)


def _norm_mod(x, g, shift, scale):
    y = x * lax.rsqrt(jnp.mean(x * x, axis=-1, keepdims=True) + EPS)
    return (y * g) * (1.0 + scale) + shift


def _dot(a, b):
    return jnp.dot(a, b, preferred_element_type=F32)


def _split_bf16(x):
    hi = x.astype(BF16)
    return hi, (x - hi.astype(F32)).astype(BF16)


def _dot_f32(a, b):
    a_hi, a_lo = _split_bf16(a)
    b_hi, b_lo = _split_bf16(b)
    rows = a.shape[0]
    top = _dot(jnp.concatenate([a_hi, a_lo], axis=0), b_hi)
    return top[:rows] + top[rows:] + _dot(a_hi, b_lo)


def _ada_kernel(cc_ref, w_ref, b_ref, o_ref):
    s = jax.nn.silu(cc_ref[...])
    o_ref[...] = _dot_f32(s, w_ref[...]) + b_ref[...]


def _ada_table(cc, ada_w, ada_b):
    depth, d, n6 = ada_w.shape
    rows = cc.shape[0]
    tn = ADA_COL_TILE
    return pl.pallas_call(
        _ada_kernel,
        grid=(depth, n6 // tn),
        in_specs=[
            pl.BlockSpec((rows, d), lambda l, n: (0, 0)),
            pl.BlockSpec((None, d, tn), lambda l, n: (l, 0, n)),
            pl.BlockSpec((None, 1, tn), lambda l, n: (l, 0, n)),
        ],
        out_specs=pl.BlockSpec((None, rows, tn), lambda l, n: (l, 0, n)),
        out_shape=jax.ShapeDtypeStruct((depth, rows, n6), F32),
        compiler_params=_cparams("arbitrary", "arbitrary"),
        name="ada_table",
    )(cc, ada_w, ada_b.reshape(depth, 1, n6))


class _Tiles:
    def __init__(self, n_batch, tm, ctx_tiles):
        self.n_batch, self.tm, self.ctx_tiles = n_batch, tm, ctx_tiles

    def mod(self, j, d):
        nb, ct = self.n_batch, self.ctx_tiles
        return pl.BlockSpec((None, None, 1, d), lambda b, t: (j, jnp.where(t < ct, nb, b), 0, 0))

    def rows(self, d, off=0):
        tm = self.tm
        return pl.BlockSpec((None, tm, d), lambda b, t: (b, t + off, 0))


def _gmlp_kernel(h_ref, sh_ref, sc_ref, gt_ref, g_ref, win_ref, vg_ref, ws_ref, bs_ref, wout_ref,
                 o_ref, gated_ref, *, width, groups):
    x = h_ref[...]
    tm = x.shape[0]
    a = _norm_mod(x, g_ref[...], sh_ref[...], sc_ref[...]).astype(BF16)
    v = jax.nn.gelu(_dot(a, win_ref[:, width:]))
    v = v * lax.rsqrt(jnp.mean(v * v, axis=-1, keepdims=True) + EPS) * vg_ref[...]
    v = v.astype(BF16)
    gd = width // groups
    per = MXU_COLS // gd
    for g0 in range(0, groups, per):
        u = jax.nn.gelu(_dot(a, win_ref[:, g0 * gd:(g0 + per) * gd]))
        for g in range(g0, g0 + per):
            cols = slice(g * gd, (g + 1) * gd)
            ug = u[:, (g - g0) * gd:(g - g0 + 1) * gd]
            for c in range(tm // CHUNK):
                rows = slice(c * CHUNK, (c + 1) * CHUNK)
                s = _dot(ws_ref[g], v[rows, cols]) + bs_ref[:, g:g + 1]
                gated_ref[rows, cols] = (ug[rows] * s).astype(BF16)
    y = _dot(gated_ref[...], wout_ref[...])
    o_ref[...] = x + gt_ref[...] * y


def _gmlp_layer(h, modl, g1, w_in, v_gain, w_s, b_s, w_out, tiles, n_tiles, off):
    n_batch, _, d = h.shape
    tm = tiles.tm
    width = w_out.shape[0]
    groups = w_s.shape[0]
    kern = functools.partial(_gmlp_kernel, width=width, groups=groups)
    return pl.pallas_call(
        kern,
        grid=(n_batch, n_tiles),
        in_specs=[
            tiles.rows(d, off), tiles.mod(0, d), tiles.mod(1, d), tiles.mod(2, d),
            _full((1, d)), _resident(w_in.shape), _full((1, width)), _resident(w_s.shape),
            _full(b_s.shape), _resident(w_out.shape),
        ],
        out_specs=tiles.rows(d),
        out_shape=jax.ShapeDtypeStruct((n_batch, n_tiles * tm, d), F32),
        scratch_shapes=[pltpu.VMEM((tm, width), BF16)],
        compiler_params=_cparams("arbitrary", "arbitrary"),
        name="gmlp_mixer",
    )(h, modl, modl, modl, g1, w_in, v_gain, w_s, b_s, w_out)


def _rope(x, cos, s_dn, s_up, quarter):
    return (x * cos + pltpu.roll(x, LANES - quarter, 1) * s_dn + pltpu.roll(x, quarter, 1) * s_up)


def _qkv_kernel(h_ref, sh_ref, sc_ref, g_ref, w_ref, qg_ref, kg_ref, cos_ref, sdn_ref, sup_ref,
                q_ref, k_ref, v_ref, *, n_q, n_k, n_v, halves, quarter, q_scale):
    x = h_ref[...]
    a = _norm_mod(x, g_ref[...], sh_ref[...], sc_ref[...]).astype(BF16)
    cos, sdn, sup = cos_ref[...], sdn_ref[...], sup_ref[...]
    lane = lax.broadcasted_iota(jnp.int32, (1, HEAD_LANES), 1)
    lo = lane < (HEAD_LANES // 2)

    def head_norm(t, gain):
        sq = t * t
        if halves == 1:
            inv = lax.rsqrt(jnp.mean(sq, axis=-1, keepdims=True) + EPS)
        else:
            tot = jnp.sum(sq, axis=-1, keepdims=True)
            s_lo = jnp.sum(jnp.where(lo, sq, 0.0), axis=-1, keepdims=True)
            half = HEAD_LANES // 2
            inv = jnp.where(lo, lax.rsqrt(s_lo / half + EPS), lax.rsqrt((tot - s_lo) / half + EPS))
        return t * inv * gain

    qg, kg = qg_ref[...], kg_ref[...]
    per = MXU_COLS // HEAD_LANES
    for j0 in range(0, n_q + n_k + n_v, per):
        t2 = _dot(a, w_ref[:, j0 * HEAD_LANES:(j0 + per) * HEAD_LANES])
        for j in range(j0, j0 + per):
            t = t2[:, (j - j0) * HEAD_LANES:(j - j0 + 1) * HEAD_LANES]
            if j < n_q:
                t = _rope(head_norm(t, qg), cos, sdn, sup, quarter) * q_scale
                q_ref[j] = t.astype(BF16)
            elif j < n_q + n_k:
                k_ref[j - n_q] = _rope(head_norm(t, kg), cos, sdn, sup, quarter).astype(BF16)
            else:
                v_ref[j - n_q - n_k] = t.astype(BF16)


def _qkv_proj(h, modl, g1, w, q_gain, k_gain, rope_tabs, tiles, n_tiles, *, n_q, n_k, n_v, halves,
              quarter, q_scale):
    n_batch, n_rows, d = h.shape
    tm = tiles.tm
    cos, sdn, sup = rope_tabs
    kern = functools.partial(_qkv_kernel, n_q=n_q, n_k=n_k, n_v=n_v, halves=halves,
                             quarter=quarter, q_scale=q_scale)
    tab = pl.BlockSpec((tm, HEAD_LANES), lambda b, t: (t, 0))

    def heads(n):
        return pl.BlockSpec((None, n, tm, HEAD_LANES), lambda b, t: (b, 0, t, 0))

    def out(n):
        return jax.ShapeDtypeStruct((n_batch, n, n_rows, HEAD_LANES), BF16)

    return pl.pallas_call(
        kern,
        grid=(n_batch, n_tiles),
        in_specs=[
            tiles.rows(d), tiles.mod(0, d), tiles.mod(1, d), _full((1, d)), _resident(w.shape),
            _full((1, HEAD_LANES)), _full((1, HEAD_LANES)), tab, tab, tab,
        ],
        out_specs=[heads(n_q), heads(n_k), heads(n_v)],
        out_shape=[out(n_q), out(n_k), out(n_v)],
        compiler_params=_cparams("arbitrary", "arbitrary"),
        name="qkv_proj",
    )(h, modl, modl, g1, w, q_gain, k_gain, cos, sdn, sup)


def _softmax_pv(q, k, v):
    s = lax.dot_general(q, k, (((1,), (1,)), ((), ())), preferred_element_type=F32)
    p = jnp.exp2(s - jnp.max(s, axis=-1, keepdims=True))
    o = _dot(p.astype(BF16), v)
    return o / jnp.sum(p, axis=-1, keepdims=True)


def _gqa_attn_kernel(q_ref, k_ref, v_ref, o_ref, *, group, n_ctx, ctx_tiles):
    qi = pl.program_id(2)

    def attend(n_keys):
        for j in range(k_ref.shape[0]):
            k, v = k_ref[j, :n_keys], v_ref[j, :n_keys]
            for g in range(group):
                head = j * group + g
                o = _softmax_pv(q_ref[head], k, v)
                o_ref[:, head * HEAD_LANES:(head + 1) * HEAD_LANES] = o.astype(BF16)

    if ctx_tiles:
        pl.when(qi < ctx_tiles)(lambda: attend(n_ctx))
        pl.when(qi >= ctx_tiles)(lambda: attend(k_ref.shape[1]))
    else:
        attend(k_ref.shape[1])


def _gqa_attention(q, k, v, tq, n_ctx, q_tiles, q_off):
    n_batch, n_heads, n_rows, _ = q.shape
    kv_heads = k.shape[1]
    group = n_heads // kv_heads
    hp = GQA_KV_HEADS_PER_STEP
    assert kv_heads % hp == 0
    ctx_tiles = 0 if q_off else n_ctx // tq
    kern = functools.partial(_gqa_attn_kernel, group=group, n_ctx=n_ctx, ctx_tiles=ctx_tiles)
    kv = pl.BlockSpec((None, hp, n_rows, HEAD_LANES), lambda b, h, t: (b, h, 0, 0))
    return pl.pallas_call(
        kern,
        grid=(n_batch, kv_heads // hp, q_tiles),
        in_specs=[
            pl.BlockSpec((None, hp * group, tq, HEAD_LANES), lambda b, h, t: (b, h, t + q_off, 0)),
            kv, kv,
        ],
        out_specs=pl.BlockSpec((None, tq, hp * group * HEAD_LANES), lambda b, h, t: (b, t, h)),
        out_shape=jax.ShapeDtypeStruct((n_batch, q_tiles * tq, n_heads * HEAD_LANES), BF16),
        compiler_params=_cparams("arbitrary", "arbitrary", "arbitrary"),
        name="gqa_attention",
    )(q, k, v)


def _diff_attn_kernel(q_ref, k_ref, v_ref, lq1_ref, lk1_ref, lq2_ref, lk2_ref, sg_ref, o_ref, *,
                      lam_init, n_ctx, ctx_tiles):
    qi = pl.program_id(2)
    lam = (jnp.exp(jnp.sum(lq1_ref[...] * lk1_ref[...], axis=-1, keepdims=True))
           - jnp.exp(jnp.sum(lq2_ref[...] * lk2_ref[...], axis=-1, keepdims=True)) + lam_init)
    lane = lax.broadcasted_iota(jnp.int32, (1, HEAD_LANES), 1)
    lo = lane < (HEAD_LANES // 2)

    def attend(n_keys):
        dims = (((1,), (1,)), ((), ()))
        for j in range(q_ref.shape[0]):
            q, k, v = q_ref[j], k_ref[j, :n_keys], v_ref[j, :n_keys]
            zero = jnp.zeros_like(q)
            s0 = lax.dot_general(jnp.where(lo, q, zero), k, dims, preferred_element_type=F32)
            s1 = lax.dot_general(jnp.where(lo, zero, q), k, dims, preferred_element_type=F32)
            e0 = jnp.exp2(s0 - jnp.max(s0, axis=-1, keepdims=True))
            e1 = jnp.exp2(s1 - jnp.max(s1, axis=-1, keepdims=True))
            c0 = 1.0 / jnp.sum(e0, axis=-1, keepdims=True)
            c1 = lam / jnp.sum(e1, axis=-1, keepdims=True)
            o = _dot(e0.astype(BF16), v) * c0 - _dot(e1.astype(BF16), v) * c1
            o = o * lax.rsqrt(jnp.mean(o * o, axis=-1, keepdims=True) + EPS) * sg_ref[...]
            o_ref[:, j * HEAD_LANES:(j + 1) * HEAD_LANES] = (o * (1.0 - lam_init)).astype(BF16)

    if ctx_tiles:
        pl.when(qi < ctx_tiles)(lambda: attend(n_ctx))
        pl.when(qi >= ctx_tiles)(lambda: attend(k_ref.shape[1]))
    else:
        attend(k_ref.shape[1])


def _diff_attention(q, k, v, lam_vecs, sub_gain, lam_init, tq, n_ctx, q_tiles, q_off):
    n_batch, n_heads, n_rows, _ = q.shape
    ctx_tiles = 0 if q_off else n_ctx // tq
    hp = DIFF_HEADS_PER_STEP
    assert n_heads % hp == 0
    kern = functools.partial(_diff_attn_kernel, lam_init=lam_init, n_ctx=n_ctx, ctx_tiles=ctx_tiles)
    kv = pl.BlockSpec((None, hp, n_rows, HEAD_LANES), lambda b, h, t: (b, h, 0, 0))
    vec = _full(lam_vecs[0].shape)
    return pl.pallas_call(
        kern,
        grid=(n_batch, n_heads // hp, q_tiles),
        in_specs=[
            pl.BlockSpec((None, hp, tq, HEAD_LANES), lambda b, h, t: (b, h, t + q_off, 0)),
            kv, kv, vec, vec, vec, vec, _full((1, HEAD_LANES)),
        ],
        out_specs=pl.BlockSpec((None, tq, hp * HEAD_LANES), lambda b, h, t: (b, t, h)),
        out_shape=jax.ShapeDtypeStruct((n_batch, q_tiles * tq, n_heads * HEAD_LANES), BF16),
        compiler_params=_cparams("arbitrary", "arbitrary", "arbitrary"),
        name="diff_attention",
    )(q, k, v, *lam_vecs, sub_gain)


def _oproj_kernel(o_ref, w_ref, h_ref, gt_ref, out_ref):
    out_ref[...] = h_ref[...] + gt_ref[...] * _dot(o_ref[...], w_ref[...])


def _out_proj(o, w, h, modl, tiles, n_tiles, h_off):
    n_batch, _, d = h.shape
    tm = tiles.tm
    return pl.pallas_call(
        _oproj_kernel,
        grid=(n_batch, n_tiles),
        in_specs=[tiles.rows(o.shape[-1]), _resident(w.shape), tiles.rows(d, h_off), tiles.mod(2, d)],
        out_specs=tiles.rows(d),
        out_shape=jax.ShapeDtypeStruct((n_batch, n_tiles * tm, d), F32),
        compiler_params=_cparams("arbitrary", "arbitrary"),
        name="out_proj",
    )(o, w, h, modl)


def _router_kernel(h_ref, sh_ref, sc_ref, g_ref, wr_ref, br_ref, mx_ref, ei_ref, wf_ref, cnt_ref,
                   carry_ref, *, n_groups, n_experts):
    first = jnp.logical_and(pl.program_id(0) == 0, pl.program_id(1) == 0)

    @pl.when(first)
    def _():
        carry_ref[...] = jnp.zeros_like(carry_ref)

    x = h_ref[...]
    tm = x.shape[0]
    mx = _norm_mod(x, g_ref[...], sh_ref[...], sc_ref[...])
    mx_ref[...] = mx
    logits = _dot_f32(mx, wr_ref[...]) + br_ref[...]
    lane = lax.broadcasted_iota(jnp.int32, logits.shape, 1).astype(F32)
    neg = jnp.float32(-jnp.inf)
    big = jnp.float32(LANES)

    def first_argmax(vals):
        m = jnp.max(vals, axis=-1, keepdims=True)
        return m, jnp.min(jnp.where(vals == m, lane, big), axis=-1, keepdims=True)

    glog = jnp.where(lane < n_groups, logits, neg)
    gmax, gidx = first_argmax(glog)
    grp_w = 1.0 / jnp.sum(jnp.exp(glog - gmax), axis=-1, keepdims=True)
    e_lo = n_groups + gidx * n_experts
    elog = jnp.where(jnp.logical_and(lane >= e_lo, lane < e_lo + n_experts), logits, neg)
    v1, i1 = first_argmax(elog)
    v2, i2 = first_argmax(jnp.where(lane == i1, neg, elog))
    t = jnp.exp(v2 - v1)
    w1 = grp_w / (1.0 + t)
    w2 = grp_w * t / (1.0 + t)

    onehot = jnp.logical_or(lane == i1, lane == i2)
    r_io = lax.broadcasted_iota(jnp.int32, (tm, tm), 0)
    c_io = lax.broadcasted_iota(jnp.int32, (tm, tm), 1)
    tri = jnp.where(r_io >= c_io, 1.0, 0.0).astype(BF16)
    incl = _dot(tri, jnp.where(onehot, 1.0, 0.0).astype(BF16))
    before = carry_ref[...] + incl - 1.0
    r1 = jnp.sum(jnp.where(lane == i1, before, 0.0), axis=-1, keepdims=True)
    r2 = jnp.sum(jnp.where(lane == i2, before, 0.0), axis=-1, keepdims=True)
    carry = carry_ref[...] + incl[tm - 1:tm, :]
    carry_ref[...] = carry
    cnt_ref[...] = carry.astype(jnp.int32)

    ei = jnp.where(lane == 0, i1 - n_groups,
                   jnp.where(lane == 1, i2 - n_groups,
                             jnp.where(lane == 2, r1, jnp.where(lane == 3, r2, 0.0))))
    ei_ref[...] = ei.astype(jnp.int32)
    wf_ref[...] = jnp.where(lane == 0, w1, jnp.where(lane == 1, w2, 0.0))


def _norm_router(h, modl, g2, w_r, b_r, tiles, n_tiles, off, n_groups, n_experts):
    n_batch, _, d = h.shape
    tm = tiles.tm
    n_tok = n_batch * n_tiles * tm
    kern = functools.partial(_router_kernel, n_groups=n_groups, n_experts=n_experts)

    def flat(width):
        return pl.BlockSpec((tm, width), lambda b, t: (b * n_tiles + t, 0))

    return pl.pallas_call(
        kern,
        grid=(n_batch, n_tiles),
        in_specs=[tiles.rows(d, off), tiles.mod(3, d), tiles.mod(4, d), _full((1, d)),
                  _full(w_r.shape), _full((1, LANES))],
        out_specs=[flat(d), flat(LANES), flat(LANES), _full((1, LANES))],
        out_shape=[
            jax.ShapeDtypeStruct((n_tok, d), F32),
            jax.ShapeDtypeStruct((n_tok, LANES), jnp.int32),
            jax.ShapeDtypeStruct((n_tok, LANES), F32),
            jax.ShapeDtypeStruct((1, LANES), jnp.int32),
        ],
        scratch_shapes=[pltpu.VMEM((1, LANES), F32)],
        compiler_params=_cparams("arbitrary", "arbitrary"),
        name="moe_norm_router",
    )(h, modl, modl, g2, w_r, b_r)


def _row_copy(src_ref, src_row, dst_ref, dst_row, sem):
    return pltpu.make_async_copy(src_ref.at[pl.ds(src_row, 1), :], dst_ref.at[pl.ds(dst_row, 1), :], sem)


def _dispatch_kernel(pos_ref, fill_ref, mx_ref, xs_ref, zero_ref, sem, fill_sem, *, n_tok, top_k, tme):
    tm = mx_ref.shape[0]
    i = pl.program_id(0)
    base = i * tm

    @pl.when(i == 0)
    def _():
        zero_ref[...] = jnp.zeros_like(zero_ref)

        def fill(wait):
            def body(j, carry):
                @pl.when(fill_ref[j] != 0)
                def _():
                    dst = xs_ref.at[pl.ds(pl.multiple_of(j * tme, tme), tme), :]
                    cp = pltpu.make_async_copy(zero_ref, dst, fill_sem)
                    cp.wait() if wait else cp.start()
                return carry
            lax.fori_loop(0, fill_ref.shape[0], body, 0)

        fill(False)
        fill(True)

    def start(r, carry):
        for kk in range(top_k):
            _row_copy(mx_ref, r, xs_ref, pos_ref[kk * n_tok + base + r], sem).start(priority=kk % 2)
        return carry

    lax.fori_loop(0, tm, start, 0, unroll=8)

    def wait(r, carry):
        for kk in range(top_k):
            _row_copy(mx_ref, 0, xs_ref, 0, sem).wait()
        return carry

    lax.fori_loop(0, tm, wait, 0, unroll=8)


def _dispatch(pos, fill, mx, n_rows, tm, top_k, tme):
    n_tok, d = mx.shape
    kern = functools.partial(_dispatch_kernel, n_tok=n_tok, top_k=top_k, tme=tme)
    return pl.pallas_call(
        kern,
        grid_spec=pltpu.PrefetchScalarGridSpec(
            num_scalar_prefetch=2,
            grid=(n_tok // tm,),
            in_specs=[pl.BlockSpec((tm, d), lambda i, pos, fill: (i, 0))],
            out_specs=pl.BlockSpec(memory_space=pl.ANY),
            scratch_shapes=[pltpu.VMEM((tme, d), F32), pltpu.SemaphoreType.DMA(()),
                            pltpu.SemaphoreType.DMA(())],
        ),
        out_shape=jax.ShapeDtypeStruct((n_rows, d), F32),
        compiler_params=_cparams("arbitrary"),
        name="moe_dispatch",
    )(pos, fill, mx)


_ACTIVE, _FIRST = 1, 2


def _ffn_kernel(te_ref, tn_ref, tx_ref, fl_ref, xs_ref, wg_hbm, wu_hbm, wd_hbm, ys_ref,
                wg_f, wu_f, wd_f, wg_b, wu_b, wd_b, sems):
    del tx_ref
    i = pl.program_id(0)
    flags = fl_ref[i]

    def weight_copies(e):
        return (pltpu.make_async_copy(wg_hbm.at[e], wg_f, sems.at[0]),
                pltpu.make_async_copy(wu_hbm.at[e], wu_f, sems.at[1]),
                pltpu.make_async_copy(wd_hbm.at[e], wd_f, sems.at[2]))

    @pl.when(jnp.logical_and(i == 0, (flags & _FIRST) != 0))
    def _():
        for cp in weight_copies(te_ref[0]):
            cp.start(priority=1)

    @pl.when((flags & _FIRST) != 0)
    def _():
        for cp in weight_copies(te_ref[i]):
            cp.wait()
        wg_b[...] = wg_f[...].astype(BF16)
        wu_b[...] = wu_f[...].astype(BF16)
        wd_b[...] = wd_f[...].astype(BF16)

        @pl.when(tn_ref[i] >= 0)
        def _():
            for cp in weight_copies(tn_ref[i]):
                cp.start(priority=1)

    @pl.when((flags & _ACTIVE) != 0)
    def _():
        x = xs_ref[...].astype(BF16)
        a = _dot(x, wg_b[...])
        b = _dot(x, wu_b[...])
        ys_ref[...] = _dot((jax.nn.silu(a) * b).astype(BF16), wd_b[...])

    @pl.when((flags & _ACTIVE) == 0)
    def _():
        ys_ref[...] = jnp.zeros_like(ys_ref)


def _expert_ffn(tile_e, tile_next, tile_x, flags, xs, w_gate, w_up, w_down, tme):
    n_rows, d = xs.shape
    n_tiles = n_rows // tme
    f = w_gate.shape[-1]
    any_spec = pl.BlockSpec(memory_space=pl.ANY)
    return pl.pallas_call(
        _ffn_kernel,
        grid_spec=pltpu.PrefetchScalarGridSpec(
            num_scalar_prefetch=4,
            grid=(n_tiles,),
            in_specs=[pl.BlockSpec((tme, d), lambda i, te, tn, tx, fl: (tx[i], 0)),
                      any_spec, any_spec, any_spec],
            out_specs=pl.BlockSpec((tme, d), lambda i, te, tn, tx, fl: (i, 0)),
            scratch_shapes=[pltpu.VMEM((d, f), F32), pltpu.VMEM((d, f), F32), pltpu.VMEM((f, d), F32),
                            pltpu.VMEM((d, f), BF16), pltpu.VMEM((d, f), BF16),
                            pltpu.VMEM((f, d), BF16), pltpu.SemaphoreType.DMA((3,))],
        ),
        out_shape=jax.ShapeDtypeStruct((n_rows, d), F32),
        compiler_params=_cparams("arbitrary"),
        name="moe_expert_ffn",
    )(tile_e, tile_next, tile_x, flags, xs, w_gate, w_up, w_down)


def _combine_kernel(pos_ref, ys_ref, h_ref, gt_ref, wf_ref, o_ref, buf, sems, *, n_tok, top_k):
    tm = h_ref.shape[0]
    i = pl.program_id(0)
    n = pl.num_programs(0)

    def gather(tile, slot, wait):
        def body(r, carry):
            for kk in range(top_k):
                src = 0 if wait else pos_ref[kk * n_tok + tile * tm + r]
                cp = pltpu.make_async_copy(ys_ref.at[pl.ds(src, 1), :],
                                           buf.at[slot, kk, pl.ds(r, 1), :], sems.at[slot])
                cp.wait() if wait else cp.start(priority=kk % 2)
            return carry
        lax.fori_loop(0, tm, body, 0, unroll=8)

    @pl.when(i == 0)
    def _():
        gather(0, 0, False)

    @pl.when(i + 1 < n)
    def _():
        gather(i + 1, (i + 1) % 2, False)

    slot = i % 2
    gather(i, slot, True)
    wf = wf_ref[...]
    y = buf[slot, 0] * wf[:, 0:1]
    for kk in range(1, top_k):
        y = y + buf[slot, kk] * wf[:, kk:kk + 1]
    o_ref[...] = h_ref[...] + gt_ref[...] * y


def _combine(pos, ys, h, modl, wf, tiles, n_tiles, off, top_k):
    n_batch, _, d = h.shape
    tm = tiles.tm
    n_tok = n_batch * n_tiles * tm
    nb, ct = n_batch, tiles.ctx_tiles
    kern = functools.partial(_combine_kernel, n_tok=n_tok, top_k=top_k)

    def bt(i):
        return i // n_tiles, i % n_tiles

    return pl.pallas_call(
        kern,
        grid_spec=pltpu.PrefetchScalarGridSpec(
            num_scalar_prefetch=1,
            grid=(n_batch * n_tiles,),
            in_specs=[
                pl.BlockSpec(memory_space=pl.ANY),
                pl.BlockSpec((None, tm, d), lambda i, pos: (bt(i)[0], bt(i)[1] + off, 0)),
                pl.BlockSpec((None, None, 1, d),
                             lambda i, pos: (5, jnp.where(bt(i)[1] < ct, nb, bt(i)[0]), 0, 0)),
                pl.BlockSpec((tm, LANES), lambda i, pos: (i, 0)),
            ],
            out_specs=pl.BlockSpec((None, tm, d), lambda i, pos: (bt(i)[0], bt(i)[1], 0)),
            scratch_shapes=[pltpu.VMEM((2, top_k, tm, d), F32), pltpu.SemaphoreType.DMA((2,))],
        ),
        out_shape=jax.ShapeDtypeStruct((n_batch, n_tiles * tm, d), F32),
        compiler_params=_cparams("arbitrary"),
        name="moe_combine",
    )(pos, ys, h, modl, wf)


def _moe_layer(layer, h, modl, g2, w_grp, b_grp, w_exp, b_exp, w_gate, w_up, w_down, tiles, n_tiles, off):
    n_batch, _, d = h.shape
    tm = tiles.tm
    n_groups, n_experts = w_gate.shape[1], w_gate.shape[2]
    n_all = n_groups * n_experts
    top_k = 2
    tme = MOE_ROW_TILE
    n_tok = n_batch * n_tiles * tm

    w_r = jnp.zeros((d, LANES), F32).at[:, :n_groups].set(w_grp).at[:, n_groups:n_groups + n_all].set(w_exp)
    b_r = jnp.zeros((1, LANES), F32).at[0, :n_groups].set(b_grp).at[0, n_groups:n_groups + n_all].set(b_exp)
    mx, ei, wf, cnt = _norm_router(h, modl, g2, w_r, b_r, tiles, n_tiles, off, n_groups, n_experts)

    counts = cnt[0, n_groups:n_groups + n_all]
    padded = ((counts + tme - 1) // tme) * tme
    ends = jnp.cumsum(padded)
    offs = ends - padded
    eid = jnp.arange(n_all, dtype=jnp.int32)
    row0 = jnp.sum(jnp.where(ei[:, :top_k, None] == eid, offs.astype(jnp.int32), 0), axis=-1)
    pos = (row0 + ei[:, top_k:2 * top_k]).T.reshape(-1).astype(jnp.int32)
    n_tile_max = (top_k * n_tok + n_all * (tme - 1)) // tme + 1
    n_rows = n_tile_max * tme
    n_active = ends[-1] // tme
    last = jnp.maximum(n_active - 1, 0)
    tidx = jnp.arange(n_tile_max, dtype=jnp.int32)
    active = tidx < n_active
    tile_x = jnp.where(active, tidx, last).astype(jnp.int32)
    tile_e = jnp.sum((ends[None, :] <= (tile_x * tme)[:, None]).astype(jnp.int32), axis=1)
    tile_e = jnp.minimum(tile_e, n_all - 1)
    prev_e = jnp.concatenate([jnp.full((1,), -1, jnp.int32), tile_e[:-1]])
    first = jnp.logical_and(active, tile_e != prev_e)
    flags = (active.astype(jnp.int32) * _ACTIVE + first.astype(jnp.int32) * _FIRST).astype(jnp.int32)
    later = jnp.logical_and(eid[None, :] > eid[:, None], (padded > 0)[None, :])
    next_e = jnp.min(jnp.where(later, eid[None, :], n_all), axis=1)
    next_e = jnp.where(next_e < n_all, next_e + layer * n_all, -1).astype(jnp.int32)
    tile_next = jnp.sum(jnp.where(tile_e[:, None] == eid[None, :], next_e[None, :], 0), axis=1)
    tile_w = (tile_e + layer * n_all).astype(jnp.int32)

    next_tile_e = jnp.concatenate([tile_e[1:], jnp.full((1,), -1, jnp.int32)])
    last_of_expert = jnp.logical_or(tile_e != next_tile_e, tidx + 1 >= n_active)
    fill = jnp.logical_or(jnp.logical_not(active), last_of_expert).astype(jnp.int32)

    xs = _dispatch(pos, fill, mx, n_rows, tm, top_k, tme)
    ys = _expert_ffn(tile_w, tile_next.astype(jnp.int32), tile_x, flags, xs,
                     w_gate.reshape(-1, d, w_gate.shape[-1]), w_up.reshape(-1, d, w_up.shape[-1]),
                     w_down.reshape(-1, w_down.shape[-2], d), tme)
    return _combine(pos, ys, h, modl, wf, tiles, n_tiles, off, top_k)


def _rope_tables(n_ctx, n_lat, head_dim):
    rows = n_lat // GRID_W
    row_id = jnp.repeat(jnp.arange(rows, dtype=F32), GRID_W)
    col_id = jnp.tile(jnp.arange(GRID_W, dtype=F32), rows)
    half = head_dim // 2
    inv_freq = ROPE_THETA ** (-jnp.arange(0, half, 2, dtype=F32) / half)
    ang_r = row_id[:, None] * inv_freq[None, :]
    ang_c = col_id[:, None] * inv_freq[None, :]
    ang = jnp.concatenate([ang_r, ang_r, ang_c, ang_c], axis=-1)
    reps = HEAD_LANES // head_dim
    cos = jnp.tile(jnp.cos(ang), (1, reps))
    sin = jnp.tile(jnp.sin(ang), (1, reps))
    quarter = head_dim // 4
    first = (jnp.arange(HEAD_LANES) % (2 * quarter)) < quarter
    s_dn = jnp.where(first[None, :], -sin, 0.0)
    s_up = jnp.where(first[None, :], 0.0, sin)
    pad = lambda t, fill: jnp.concatenate([jnp.full((n_ctx, HEAD_LANES), fill, F32), t], axis=0)
    return pad(cos, 1.0), pad(s_dn, 0.0), pad(s_up, 0.0)


def kernel(x, c, ctx, c_ctx, ada_w, ada_b, norm1_g, norm2_g, gmlp_w_in, gmlp_v_gain, gmlp_w_s, gmlp_b_s, gmlp_w_out, gqa_w_qkv, gqa_q_gain, gqa_k_gain, gqa_w_o, diff_w_qkv, diff_q_gain, diff_k_gain, diff_lam_q1, diff_lam_k1, diff_lam_q2, diff_lam_k2, diff_sub_gain, diff_w_o, moe_w_grp, moe_b_grp, moe_w_exp, moe_b_exp, moe_w_gate, moe_w_up, moe_w_down):
    n_batch, n_lat, d = x.shape
    n_ctx = ctx.shape[1]
    depth = ada_w.shape[0]
    tm = 256 if (n_ctx % 256 == 0 and n_lat % 256 == 0) else 128
    assert n_ctx % tm == 0 and n_lat % tm == 0 and tm % CHUNK == 0 and d % LANES == 0
    ctx_tiles, lat_tiles = n_ctx // tm, n_lat // tm

    mod_rows = -(-(n_batch + 1) // 8) * 8
    cc = jnp.zeros((mod_rows, d), F32).at[:n_batch].set(c).at[n_batch].set(c_ctx)
    mod = _ada_table(cc, ada_w, ada_b)
    mod = mod.reshape(depth, mod_rows, 6, 1, d).transpose(0, 2, 1, 3, 4)

    gqa_heads = gqa_w_o.shape[1] // HEAD_LANES
    gqa_kv_heads = (gqa_w_qkv.shape[2] // HEAD_LANES - gqa_heads) // 2
    diff_heads = diff_w_o.shape[1] // HEAD_LANES
    rope_b = _rope_tables(n_ctx, n_lat, HEAD_LANES)
    rope_c = _rope_tables(n_ctx, n_lat, HEAD_LANES // 2)

    h = jnp.concatenate([ctx, x], axis=1)
    has_ctx = True
    for i in range(depth):
        kind, j = i % N_MIXERS, i // N_MIXERS
        need_ctx = any((k % N_MIXERS) != 0 for k in range(i + 1, depth))
        modl = mod[i]
        g1 = norm1_g[i].reshape(1, d)
        g2 = norm2_g[i].reshape(1, d)
        in_tiles = _Tiles(n_batch, tm, ctx_tiles if has_ctx else 0)
        n_in = (ctx_tiles if has_ctx else 0) + lat_tiles
        keep = has_ctx and need_ctx
        out_tiles = _Tiles(n_batch, tm, ctx_tiles if keep else 0)
        n_out = (ctx_tiles if keep else 0) + lat_tiles
        off = n_in - n_out

        if kind == 0:
            h = _gmlp_layer(h, modl, g1, gmlp_w_in[j].astype(BF16), gmlp_v_gain[j].reshape(1, -1),
                            gmlp_w_s[j].astype(BF16), gmlp_b_s[j].T, gmlp_w_out[j].astype(BF16),
                            out_tiles, n_out, off)
        else:
            if kind == 1:
                q, k, v = _qkv_proj(
                    h, modl, g1, gqa_w_qkv[j].astype(BF16), gqa_q_gain[j].reshape(1, -1),
                    gqa_k_gain[j].reshape(1, -1), rope_b if has_ctx else tuple(t[n_ctx:] for t in rope_b),
                    in_tiles, n_in, n_q=gqa_heads, n_k=gqa_kv_heads, n_v=gqa_kv_heads, halves=1,
                    quarter=HEAD_LANES // 4, q_scale=LOG2E / math.sqrt(HEAD_LANES))
                o = _gqa_attention(q, k, v, tm, n_ctx if has_ctx else 0, n_out, off)
                w_o = gqa_w_o[j]
            else:
                lam_init = 0.8 - 0.6 * math.exp(-0.3 * i)
                q, k, v = _qkv_proj(
                    h, modl, g1, diff_w_qkv[j].astype(BF16), diff_q_gain[j].reshape(1, -1),
                    diff_k_gain[j].reshape(1, -1), rope_c if has_ctx else tuple(t[n_ctx:] for t in rope_c),
                    in_tiles, n_in, n_q=diff_heads, n_k=diff_heads, n_v=diff_heads, halves=2,
                    quarter=HEAD_LANES // 8, q_scale=LOG2E / math.sqrt(HEAD_LANES // 2))
                lam_vecs = [t[j].reshape(1, -1) for t in (diff_lam_q1, diff_lam_k1, diff_lam_q2, diff_lam_k2)]
                o = _diff_attention(q, k, v, lam_vecs, diff_sub_gain[j].reshape(1, -1), lam_init, tm,
                                    n_ctx if has_ctx else 0, n_out, off)
                w_o = diff_w_o[j]
            h = _out_proj(o, w_o.astype(BF16), h, modl, out_tiles, n_out, off)

        h = _moe_layer(i, h, modl, g2, moe_w_grp[i], moe_b_grp[i], moe_w_exp[i], moe_b_exp[i],
                       moe_w_gate, moe_w_up, moe_w_down, out_tiles, n_out, 0)
        has_ctx = keep
    return h[:, n_ctx:] if has_ctx else h
```

```python
import functools
import math

import jax
import jax.numpy as jnp
from jax import lax
from jax.experimental import pallas as pl
from jax.experimental.pallas import tpu as pltpu

EPS = 1e-6
GRID_W = 64
ROPE_THETA = 10000.0
N_MIXERS = 3
CHUNK = 128
LANES = 128
HEAD_LANES = 128
MXU_COLS = 256
MOE_ROW_TILE = 128
DIFF_HEADS_PER_STEP = 4
GQA_KV_HEADS_PER_STEP = 4
VMEM_LIMIT_BYTES = 60000 * 1024
ADA_COL_TILE = 1024
LOG2E = math.log2(math.e)

F32 = jnp.float32
BF16 = jnp.bfloat16


def _cparams(*sem):
    return pltpu.CompilerParams(dimension_semantics=sem, vmem_limit_bytes=VMEM_LIMIT_BYTES)


def _resident(shape):
    nd = len(shape)
    return pl.BlockSpec(shape, lambda *_: (0,) * nd, pipeline_mode=pl.Buffered(1))


def _full(shape):
    nd = len(shape)
    return pl.BlockSpec(shape, lambda *_: (0,) * nd)


def _norm_mod(x, g, shift, scale):
    y = x * lax.rsqrt(jnp.mean(x * x, axis=-1, keepdims=True) + EPS)
    return (y * g) * (1.0 + scale) + shift


def _dot(a, b):
    return jnp.dot(a, b, preferred_element_type=F32)


def _split_bf16(x):
    hi = x.astype(BF16)
    return hi, (x - hi.astype(F32)).astype(BF16)


def _dot_f32(a, b):
    a_hi, a_lo = _split_bf16(a)
    b_hi, b_lo = _split_bf16(b)
    rows = a.shape[0]
    top = _dot(jnp.concatenate([a_hi, a_lo], axis=0), b_hi)
    return top[:rows] + top[rows:] + _dot(a_hi, b_lo)


def _ada_kernel(cc_ref, w_ref, b_ref, o_ref):
    s = jax.nn.silu(cc_ref[...])
    o_ref[...] = _dot_f32(s, w_ref[...]) + b_ref[...]


def _ada_table(cc, ada_w, ada_b):
    depth, d, n6 = ada_w.shape
    rows = cc.shape[0]
    tn = ADA_COL_TILE
    return pl.pallas_call(
        _ada_kernel,
        grid=(depth, n6 // tn),
        in_specs=[
            pl.BlockSpec((rows, d), lambda l, n: (0, 0)),
            pl.BlockSpec((None, d, tn), lambda l, n: (l, 0, n)),
            pl.BlockSpec((None, 1, tn), lambda l, n: (l, 0, n)),
        ],
        out_specs=pl.BlockSpec((None, rows, tn), lambda l, n: (l, 0, n)),
        out_shape=jax.ShapeDtypeStruct((depth, rows, n6), F32),
        compiler_params=_cparams("arbitrary", "arbitrary"),
        name="ada_table",
    )(cc, ada_w, ada_b.reshape(depth, 1, n6))


class _Tiles:
    def __init__(self, n_batch, tm, ctx_tiles):
        self.n_batch, self.tm, self.ctx_tiles = n_batch, tm, ctx_tiles

    def mod(self, j, d):
        nb, ct = self.n_batch, self.ctx_tiles
        return pl.BlockSpec((None, None, 1, d), lambda b, t: (j, jnp.where(t < ct, nb, b), 0, 0))

    def rows(self, d, off=0):
        tm = self.tm
        return pl.BlockSpec((None, tm, d), lambda b, t: (b, t + off, 0))


def _gmlp_kernel(h_ref, sh_ref, sc_ref, gt_ref, g_ref, win_ref, vg_ref, ws_ref, bs_ref, wout_ref,
                 o_ref, gated_ref, *, width, groups):
    x = h_ref[...]
    tm = x.shape[0]
    a = _norm_mod(x, g_ref[...], sh_ref[...], sc_ref[...]).astype(BF16)
    v = jax.nn.gelu(_dot(a, win_ref[:, width:]))
    v = v * lax.rsqrt(jnp.mean(v * v, axis=-1, keepdims=True) + EPS) * vg_ref[...]
    v = v.astype(BF16)
    gd = width // groups
    per = MXU_COLS // gd
    for g0 in range(0, groups, per):
        u = jax.nn.gelu(_dot(a, win_ref[:, g0 * gd:(g0 + per) * gd]))
        for g in range(g0, g0 + per):
            cols = slice(g * gd, (g + 1) * gd)
            ug = u[:, (g - g0) * gd:(g - g0 + 1) * gd]
            for c in range(tm // CHUNK):
                rows = slice(c * CHUNK, (c + 1) * CHUNK)
                s = _dot(ws_ref[g], v[rows, cols]) + bs_ref[:, g:g + 1]
                gated_ref[rows, cols] = (ug[rows] * s).astype(BF16)
    y = _dot(gated_ref[...], wout_ref[...])
    o_ref[...] = x + gt_ref[...] * y


def _gmlp_layer(h, modl, g1, w_in, v_gain, w_s, b_s, w_out, tiles, n_tiles, off):
    n_batch, _, d = h.shape
    tm = tiles.tm
    width = w_out.shape[0]
    groups = w_s.shape[0]
    kern = functools.partial(_gmlp_kernel, width=width, groups=groups)
    return pl.pallas_call(
        kern,
        grid=(n_batch, n_tiles),
        in_specs=[
            tiles.rows(d, off), tiles.mod(0, d), tiles.mod(1, d), tiles.mod(2, d),
            _full((1, d)), _resident(w_in.shape), _full((1, width)), _resident(w_s.shape),
            _full(b_s.shape), _resident(w_out.shape),
        ],
        out_specs=tiles.rows(d),
        out_shape=jax.ShapeDtypeStruct((n_batch, n_tiles * tm, d), F32),
        scratch_shapes=[pltpu.VMEM((tm, width), BF16)],
        compiler_params=_cparams("arbitrary", "arbitrary"),
        name="gmlp_mixer",
    )(h, modl, modl, modl, g1, w_in, v_gain, w_s, b_s, w_out)


def _rope(x, cos, s_dn, s_up, quarter):
    return (x * cos + pltpu.roll(x, LANES - quarter, 1) * s_dn + pltpu.roll(x, quarter, 1) * s_up)


def _qkv_kernel(h_ref, sh_ref, sc_ref, g_ref, w_ref, qg_ref, kg_ref, cos_ref, sdn_ref, sup_ref,
                q_ref, k_ref, v_ref, *, n_q, n_k, n_v, halves, quarter, q_scale):
    x = h_ref[...]
    a = _norm_mod(x, g_ref[...], sh_ref[...], sc_ref[...]).astype(BF16)
    cos, sdn, sup = cos_ref[...], sdn_ref[...], sup_ref[...]
    lane = lax.broadcasted_iota(jnp.int32, (1, HEAD_LANES), 1)
    lo = lane < (HEAD_LANES // 2)

    def head_norm(t, gain):
        sq = t * t
        if halves == 1:
            inv = lax.rsqrt(jnp.mean(sq, axis=-1, keepdims=True) + EPS)
        else:
            tot = jnp.sum(sq, axis=-1, keepdims=True)
            s_lo = jnp.sum(jnp.where(lo, sq, 0.0), axis=-1, keepdims=True)
            half = HEAD_LANES // 2
            inv = jnp.where(lo, lax.rsqrt(s_lo / half + EPS), lax.rsqrt((tot - s_lo) / half + EPS))
        return t * inv * gain

    qg, kg = qg_ref[...], kg_ref[...]
    per = MXU_COLS // HEAD_LANES
    for j0 in range(0, n_q + n_k + n_v, per):
        t2 = _dot(a, w_ref[:, j0 * HEAD_LANES:(j0 + per) * HEAD_LANES])
        for j in range(j0, j0 + per):
            t = t2[:, (j - j0) * HEAD_LANES:(j - j0 + 1) * HEAD_LANES]
            if j < n_q:
                t = _rope(head_norm(t, qg), cos, sdn, sup, quarter) * q_scale
                q_ref[j] = t.astype(BF16)
            elif j < n_q + n_k:
                k_ref[j - n_q] = _rope(head_norm(t, kg), cos, sdn, sup, quarter).astype(BF16)
            else:
                v_ref[j - n_q - n_k] = t.astype(BF16)


def _qkv_proj(h, modl, g1, w, q_gain, k_gain, rope_tabs, tiles, n_tiles, *, n_q, n_k, n_v, halves,
              quarter, q_scale):
    n_batch, n_rows, d = h.shape
    tm = tiles.tm
    cos, sdn, sup = rope_tabs
    kern = functools.partial(_qkv_kernel, n_q=n_q, n_k=n_k, n_v=n_v, halves=halves,
                             quarter=quarter, q_scale=q_scale)
    tab = pl.BlockSpec((tm, HEAD_LANES), lambda b, t: (t, 0))

    def heads(n):
        return pl.BlockSpec((None, n, tm, HEAD_LANES), lambda b, t: (b, 0, t, 0))

    def out(n):
        return jax.ShapeDtypeStruct((n_batch, n, n_rows, HEAD_LANES), BF16)

    return pl.pallas_call(
        kern,
        grid=(n_batch, n_tiles),
        in_specs=[
            tiles.rows(d), tiles.mod(0, d), tiles.mod(1, d), _full((1, d)), _resident(w.shape),
            _full((1, HEAD_LANES)), _full((1, HEAD_LANES)), tab, tab, tab,
        ],
        out_specs=[heads(n_q), heads(n_k), heads(n_v)],
        out_shape=[out(n_q), out(n_k), out(n_v)],
        compiler_params=_cparams("arbitrary", "arbitrary"),
        name="qkv_proj",
    )(h, modl, modl, g1, w, q_gain, k_gain, cos, sdn, sup)


def _softmax_pv(q, k, v):
    s = lax.dot_general(q, k, (((1,), (1,)), ((), ())), preferred_element_type=F32)
    p = jnp.exp2(s - jnp.max(s, axis=-1, keepdims=True))
    o = _dot(p.astype(BF16), v)
    return o / jnp.sum(p, axis=-1, keepdims=True)


def _gqa_attn_kernel(q_ref, k_ref, v_ref, o_ref, *, group, n_ctx, ctx_tiles):
    qi = pl.program_id(2)

    def attend(n_keys):
        for j in range(k_ref.shape[0]):
            k, v = k_ref[j, :n_keys], v_ref[j, :n_keys]
            for g in range(group):
                head = j * group + g
                o = _softmax_pv(q_ref[head], k, v)
                o_ref[:, head * HEAD_LANES:(head + 1) * HEAD_LANES] = o.astype(BF16)

    if ctx_tiles:
        pl.when(qi < ctx_tiles)(lambda: attend(n_ctx))
        pl.when(qi >= ctx_tiles)(lambda: attend(k_ref.shape[1]))
    else:
        attend(k_ref.shape[1])


def _gqa_attention(q, k, v, tq, n_ctx, q_tiles, q_off):
    n_batch, n_heads, n_rows, _ = q.shape
    kv_heads = k.shape[1]
    group = n_heads // kv_heads
    hp = GQA_KV_HEADS_PER_STEP
    assert kv_heads % hp == 0
    ctx_tiles = 0 if q_off else n_ctx // tq
    kern = functools.partial(_gqa_attn_kernel, group=group, n_ctx=n_ctx, ctx_tiles=ctx_tiles)
    kv = pl.BlockSpec((None, hp, n_rows, HEAD_LANES), lambda b, h, t: (b, h, 0, 0))
    return pl.pallas_call(
        kern,
        grid=(n_batch, kv_heads // hp, q_tiles),
        in_specs=[
            pl.BlockSpec((None, hp * group, tq, HEAD_LANES), lambda b, h, t: (b, h, t + q_off, 0)),
            kv, kv,
        ],
        out_specs=pl.BlockSpec((None, tq, hp * group * HEAD_LANES), lambda b, h, t: (b, t, h)),
        out_shape=jax.ShapeDtypeStruct((n_batch, q_tiles * tq, n_heads * HEAD_LANES), BF16),
        compiler_params=_cparams("arbitrary", "arbitrary", "arbitrary"),
        name="gqa_attention",
    )(q, k, v)


def _diff_attn_kernel(q_ref, k_ref, v_ref, lq1_ref, lk1_ref, lq2_ref, lk2_ref, sg_ref, o_ref, *,
                      lam_init, n_ctx, ctx_tiles):
    qi = pl.program_id(2)
    lam = (jnp.exp(jnp.sum(lq1_ref[...] * lk1_ref[...], axis=-1, keepdims=True))
           - jnp.exp(jnp.sum(lq2_ref[...] * lk2_ref[...], axis=-1, keepdims=True)) + lam_init)
    lane = lax.broadcasted_iota(jnp.int32, (1, HEAD_LANES), 1)
    lo = lane < (HEAD_LANES // 2)

    def attend(n_keys):
        dims = (((1,), (1,)), ((), ()))
        for j in range(q_ref.shape[0]):
            q, k, v = q_ref[j], k_ref[j, :n_keys], v_ref[j, :n_keys]
            zero = jnp.zeros_like(q)
            s0 = lax.dot_general(jnp.where(lo, q, zero), k, dims, preferred_element_type=F32)
            s1 = lax.dot_general(jnp.where(lo, zero, q), k, dims, preferred_element_type=F32)
            e0 = jnp.exp2(s0 - jnp.max(s0, axis=-1, keepdims=True))
            e1 = jnp.exp2(s1 - jnp.max(s1, axis=-1, keepdims=True))
            c0 = 1.0 / jnp.sum(e0, axis=-1, keepdims=True)
            c1 = lam / jnp.sum(e1, axis=-1, keepdims=True)
            o = _dot(e0.astype(BF16), v) * c0 - _dot(e1.astype(BF16), v) * c1
            o = o * lax.rsqrt(jnp.mean(o * o, axis=-1, keepdims=True) + EPS) * sg_ref[...]
            o_ref[:, j * HEAD_LANES:(j + 1) * HEAD_LANES] = (o * (1.0 - lam_init)).astype(BF16)

    if ctx_tiles:
        pl.when(qi < ctx_tiles)(lambda: attend(n_ctx))
        pl.when(qi >= ctx_tiles)(lambda: attend(k_ref.shape[1]))
    else:
        attend(k_ref.shape[1])


def _diff_attention(q, k, v, lam_vecs, sub_gain, lam_init, tq, n_ctx, q_tiles, q_off):
    n_batch, n_heads, n_rows, _ = q.shape
    ctx_tiles = 0 if q_off else n_ctx // tq
    hp = DIFF_HEADS_PER_STEP
    assert n_heads % hp == 0
    kern = functools.partial(_diff_attn_kernel, lam_init=lam_init, n_ctx=n_ctx, ctx_tiles=ctx_tiles)
    kv = pl.BlockSpec((None, hp, n_rows, HEAD_LANES), lambda b, h, t: (b, h, 0, 0))
    vec = _full(lam_vecs[0].shape)
    return pl.pallas_call(
        kern,
        grid=(n_batch, n_heads // hp, q_tiles),
        in_specs=[
            pl.BlockSpec((None, hp, tq, HEAD_LANES), lambda b, h, t: (b, h, t + q_off, 0)),
            kv, kv, vec, vec, vec, vec, _full((1, HEAD_LANES)),
        ],
        out_specs=pl.BlockSpec((None, tq, hp * HEAD_LANES), lambda b, h, t: (b, t, h)),
        out_shape=jax.ShapeDtypeStruct((n_batch, q_tiles * tq, n_heads * HEAD_LANES), BF16),
        compiler_params=_cparams("arbitrary", "arbitrary", "arbitrary"),
        name="diff_attention",
    )(q, k, v, *lam_vecs, sub_gain)


def _oproj_kernel(o_ref, w_ref, h_ref, gt_ref, out_ref):
    out_ref[...] = h_ref[...] + gt_ref[...] * _dot(o_ref[...], w_ref[...])


def _out_proj(o, w, h, modl, tiles, n_tiles, h_off):
    n_batch, _, d = h.shape
    tm = tiles.tm
    return pl.pallas_call(
        _oproj_kernel,
        grid=(n_batch, n_tiles),
        in_specs=[tiles.rows(o.shape[-1]), _resident(w.shape), tiles.rows(d, h_off), tiles.mod(2, d)],
        out_specs=tiles.rows(d),
        out_shape=jax.ShapeDtypeStruct((n_batch, n_tiles * tm, d), F32),
        compiler_params=_cparams("arbitrary", "arbitrary"),
        name="out_proj",
    )(o, w, h, modl)


def _router_kernel(h_ref, sh_ref, sc_ref, g_ref, wr_ref, br_ref, mx_ref, ei_ref, wf_ref, cnt_ref,
                   carry_ref, *, n_groups, n_experts):
    first = jnp.logical_and(pl.program_id(0) == 0, pl.program_id(1) == 0)

    @pl.when(first)
    def _():
        carry_ref[...] = jnp.zeros_like(carry_ref)

    x = h_ref[...]
    tm = x.shape[0]
    mx = _norm_mod(x, g_ref[...], sh_ref[...], sc_ref[...])
    mx_ref[...] = mx
    logits = _dot_f32(mx, wr_ref[...]) + br_ref[...]
    lane = lax.broadcasted_iota(jnp.int32, logits.shape, 1).astype(F32)
    neg = jnp.float32(-jnp.inf)
    big = jnp.float32(LANES)

    def first_argmax(vals):
        m = jnp.max(vals, axis=-1, keepdims=True)
        return m, jnp.min(jnp.where(vals == m, lane, big), axis=-1, keepdims=True)

    glog = jnp.where(lane < n_groups, logits, neg)
    gmax, gidx = first_argmax(glog)
    grp_w = 1.0 / jnp.sum(jnp.exp(glog - gmax), axis=-1, keepdims=True)
    e_lo = n_groups + gidx * n_experts
    elog = jnp.where(jnp.logical_and(lane >= e_lo, lane < e_lo + n_experts), logits, neg)
    v1, i1 = first_argmax(elog)
    v2, i2 = first_argmax(jnp.where(lane == i1, neg, elog))
    t = jnp.exp(v2 - v1)
    w1 = grp_w / (1.0 + t)
    w2 = grp_w * t / (1.0 + t)

    onehot = jnp.logical_or(lane == i1, lane == i2)
    r_io = lax.broadcasted_iota(jnp.int32, (tm, tm), 0)
    c_io = lax.broadcasted_iota(jnp.int32, (tm, tm), 1)
    tri = jnp.where(r_io >= c_io, 1.0, 0.0).astype(BF16)
    incl = _dot(tri, jnp.where(onehot, 1.0, 0.0).astype(BF16))
    before = carry_ref[...] + incl - 1.0
    r1 = jnp.sum(jnp.where(lane == i1, before, 0.0), axis=-1, keepdims=True)
    r2 = jnp.sum(jnp.where(lane == i2, before, 0.0), axis=-1, keepdims=True)
    carry = carry_ref[...] + incl[tm - 1:tm, :]
    carry_ref[...] = carry
    cnt_ref[...] = carry.astype(jnp.int32)

    ei = jnp.where(lane == 0, i1 - n_groups,
                   jnp.where(lane == 1, i2 - n_groups,
                             jnp.where(lane == 2, r1, jnp.where(lane == 3, r2, 0.0))))
    ei_ref[...] = ei.astype(jnp.int32)
    wf_ref[...] = jnp.where(lane == 0, w1, jnp.where(lane == 1, w2, 0.0))


def _norm_router(h, modl, g2, w_r, b_r, tiles, n_tiles, off, n_groups, n_experts):
    n_batch, _, d = h.shape
    tm = tiles.tm
    n_tok = n_batch * n_tiles * tm
    kern = functools.partial(_router_kernel, n_groups=n_groups, n_experts=n_experts)

    def flat(width):
        return pl.BlockSpec((tm, width), lambda b, t: (b * n_tiles + t, 0))

    return pl.pallas_call(
        kern,
        grid=(n_batch, n_tiles),
        in_specs=[tiles.rows(d, off), tiles.mod(3, d), tiles.mod(4, d), _full((1, d)),
                  _full(w_r.shape), _full((1, LANES))],
        out_specs=[flat(d), flat(LANES), flat(LANES), _full((1, LANES))],
        out_shape=[
            jax.ShapeDtypeStruct((n_tok, d), F32),
            jax.ShapeDtypeStruct((n_tok, LANES), jnp.int32),
            jax.ShapeDtypeStruct((n_tok, LANES), F32),
            jax.ShapeDtypeStruct((1, LANES), jnp.int32),
        ],
        scratch_shapes=[pltpu.VMEM((1, LANES), F32)],
        compiler_params=_cparams("arbitrary", "arbitrary"),
        name="moe_norm_router",
    )(h, modl, modl, g2, w_r, b_r)


SUBLANES = 8


def _dispatch_kernel(pos_ref, fill_ref, mx_ref, xs_ref, zero_ref, sem, fill_sem, *, n_tok, top_k, tme):
    tm = mx_ref.shape[0] * SUBLANES
    i = pl.program_id(0)
    base = i * tm

    @pl.when(i == 0)
    def _():
        zero_ref[...] = jnp.zeros_like(zero_ref)

        def fill(wait):
            def body(j, carry):
                @pl.when(fill_ref[j] != 0)
                def _():
                    dst = xs_ref.at[pl.ds(pl.multiple_of(j * tme, tme), tme), :]
                    cp = pltpu.make_async_copy(zero_ref, dst, fill_sem)
                    cp.wait() if wait else cp.start()
                return carry
            lax.fori_loop(0, fill_ref.shape[0], body, 0)

        fill(False)
        fill(True)

    def rows(g, wait):
        for u in range(SUBLANES):
            for kk in range(top_k):
                dst = 0 if wait else pos_ref[kk * n_tok + base + g * SUBLANES + u]
                cp = pltpu.make_async_copy(mx_ref.at[g, pl.ds(u, 1), :],
                                           xs_ref.at[pl.ds(dst, 1), :], sem)
                cp.wait() if wait else cp.start(priority=kk % 2)

    def start(g, carry):
        rows(g, False)
        return carry

    def wait(g, carry):
        rows(0, True)
        return carry

    lax.fori_loop(0, tm // SUBLANES, start, 0)
    lax.fori_loop(0, tm // SUBLANES, wait, 0)


def _dispatch(pos, fill, mx, n_rows, tm, top_k, tme):
    n_tok, d = mx.shape
    kern = functools.partial(_dispatch_kernel, n_tok=n_tok, top_k=top_k, tme=tme)
    return pl.pallas_call(
        kern,
        grid_spec=pltpu.PrefetchScalarGridSpec(
            num_scalar_prefetch=2,
            grid=(n_tok // tm,),
            in_specs=[pl.BlockSpec((tm // SUBLANES, SUBLANES, d), lambda i, pos, fill: (i, 0, 0))],
            out_specs=pl.BlockSpec(memory_space=pl.ANY),
            scratch_shapes=[pltpu.VMEM((tme, d), F32), pltpu.SemaphoreType.DMA(()),
                            pltpu.SemaphoreType.DMA(())],
        ),
        out_shape=jax.ShapeDtypeStruct((n_rows, d), F32),
        compiler_params=_cparams("arbitrary"),
        name="moe_dispatch",
    )(pos, fill, mx.reshape(n_tok // SUBLANES, SUBLANES, d))


_ACTIVE, _FIRST, _SLOT = 1, 2, 4


def _ffn_kernel(te_ref, tn_ref, tx_ref, fl_ref, xs_ref, wg_hbm, wu_hbm, wd_hbm, ys_ref,
                wg_f, wu_f, wd_f, wg_b, wu_b, wd_b, sems):
    del tx_ref
    i = pl.program_id(0)
    flags = fl_ref[i]

    slot = jnp.where((flags & _SLOT) != 0, 1, 0)

    def weight_copies(e, s):
        return (pltpu.make_async_copy(wg_hbm.at[e], wg_f.at[s], sems.at[s, 0]),
                pltpu.make_async_copy(wu_hbm.at[e], wu_f.at[s], sems.at[s, 1]),
                pltpu.make_async_copy(wd_hbm.at[e], wd_f.at[s], sems.at[s, 2]))

    @pl.when(jnp.logical_and(i == 0, (flags & _FIRST) != 0))
    def _():
        for cp in weight_copies(te_ref[0], slot):
            cp.start(priority=1)

    @pl.when((flags & _FIRST) != 0)
    def _():
        for cp in weight_copies(te_ref[i], slot):
            cp.wait()

        @pl.when(tn_ref[i] >= 0)
        def _():
            for cp in weight_copies(tn_ref[i], 1 - slot):
                cp.start(priority=1)

        wg_b[...] = wg_f[slot].astype(BF16)
        wu_b[...] = wu_f[slot].astype(BF16)
        wd_b[...] = wd_f[slot].astype(BF16)

    @pl.when((flags & _ACTIVE) != 0)
    def _():
        x = xs_ref[...].astype(BF16)
        a = _dot(x, wg_b[...])
        b = _dot(x, wu_b[...])
        ys_ref[...] = _dot((jax.nn.silu(a) * b).astype(BF16), wd_b[...])

    @pl.when((flags & _ACTIVE) == 0)
    def _():
        ys_ref[...] = jnp.zeros_like(ys_ref)


def _expert_ffn(tile_e, tile_next, tile_x, flags, xs, w_gate, w_up, w_down, tme):
    n_rows, d = xs.shape
    n_tiles = n_rows // tme
    f = w_gate.shape[-1]
    any_spec = pl.BlockSpec(memory_space=pl.ANY)
    return pl.pallas_call(
        _ffn_kernel,
        grid_spec=pltpu.PrefetchScalarGridSpec(
            num_scalar_prefetch=4,
            grid=(n_tiles,),
            in_specs=[pl.BlockSpec((tme, d), lambda i, te, tn, tx, fl: (tx[i], 0)),
                      any_spec, any_spec, any_spec],
            out_specs=pl.BlockSpec((tme, d), lambda i, te, tn, tx, fl: (i, 0)),
            scratch_shapes=[pltpu.VMEM((2, d, f), F32), pltpu.VMEM((2, d, f), F32),
                            pltpu.VMEM((2, f, d), F32),
                            pltpu.VMEM((d, f), BF16), pltpu.VMEM((d, f), BF16),
                            pltpu.VMEM((f, d), BF16), pltpu.SemaphoreType.DMA((2, 3))],
        ),
        out_shape=jax.ShapeDtypeStruct((n_rows, d), F32),
        compiler_params=_cparams("arbitrary"),
        name="moe_expert_ffn",
    )(tile_e, tile_next, tile_x, flags, xs, w_gate, w_up, w_down)


def _combine_kernel(pos_ref, ys_ref, h_ref, gt_ref, wf_ref, o_ref, buf, sems, *, n_tok, top_k):
    tm = h_ref.shape[0]
    i = pl.program_id(0)
    n = pl.num_programs(0)

    def gather(tile, slot, wait):
        def body(g, carry):
            for u in range(SUBLANES):
                for kk in range(top_k):
                    src = 0 if wait else pos_ref[kk * n_tok + tile * tm + g * SUBLANES + u]
                    cp = pltpu.make_async_copy(ys_ref.at[pl.ds(src, 1), :],
                                               buf.at[slot, kk, 0 if wait else g, pl.ds(u, 1), :],
                                               sems.at[slot])
                    cp.wait() if wait else cp.start(priority=kk % 2)
            return carry
        lax.fori_loop(0, tm // SUBLANES, body, 0)

    @pl.when(i == 0)
    def _():
        gather(0, 0, False)

    @pl.when(i + 1 < n)
    def _():
        gather(i + 1, (i + 1) % 2, False)

    slot = i % 2
    gather(i, slot, True)
    wf = wf_ref[...]
    d = h_ref.shape[1]
    y = buf[slot, 0].reshape(tm, d) * wf[:, 0:1]
    for kk in range(1, top_k):
        y = y + buf[slot, kk].reshape(tm, d) * wf[:, kk:kk + 1]
    o_ref[...] = h_ref[...] + gt_ref[...] * y


def _combine(pos, ys, h, modl, wf, tiles, n_tiles, off, top_k):
    n_batch, _, d = h.shape
    tm = tiles.tm
    n_tok = n_batch * n_tiles * tm
    nb, ct = n_batch, tiles.ctx_tiles
    kern = functools.partial(_combine_kernel, n_tok=n_tok, top_k=top_k)

    def bt(i):
        return i // n_tiles, i % n_tiles

    return pl.pallas_call(
        kern,
        grid_spec=pltpu.PrefetchScalarGridSpec(
            num_scalar_prefetch=1,
            grid=(n_batch * n_tiles,),
            in_specs=[
                pl.BlockSpec(memory_space=pl.ANY),
                pl.BlockSpec((None, tm, d), lambda i, pos: (bt(i)[0], bt(i)[1] + off, 0)),
                pl.BlockSpec((None, None, 1, d),
                             lambda i, pos: (5, jnp.where(bt(i)[1] < ct, nb, bt(i)[0]), 0, 0)),
                pl.BlockSpec((tm, LANES), lambda i, pos: (i, 0)),
            ],
            out_specs=pl.BlockSpec((None, tm, d), lambda i, pos: (bt(i)[0], bt(i)[1], 0)),
            scratch_shapes=[pltpu.VMEM((2, top_k, tm // SUBLANES, SUBLANES, d), F32),
                            pltpu.SemaphoreType.DMA((2,))],
        ),
        out_shape=jax.ShapeDtypeStruct((n_batch, n_tiles * tm, d), F32),
        compiler_params=_cparams("arbitrary"),
        name="moe_combine",
    )(pos, ys, h, modl, wf)


def _moe_layer(layer, h, modl, g2, w_grp, b_grp, w_exp, b_exp, w_gate, w_up, w_down, tiles, n_tiles, off):
    n_batch, _, d = h.shape
    tm = tiles.tm
    n_groups, n_experts = w_gate.shape[1], w_gate.shape[2]
    n_all = n_groups * n_experts
    top_k = 2
    tme = MOE_ROW_TILE
    n_tok = n_batch * n_tiles * tm

    w_r = jnp.zeros((d, LANES), F32).at[:, :n_groups].set(w_grp).at[:, n_groups:n_groups + n_all].set(w_exp)
    b_r = jnp.zeros((1, LANES), F32).at[0, :n_groups].set(b_grp).at[0, n_groups:n_groups + n_all].set(b_exp)
    mx, ei, wf, cnt = _norm_router(h, modl, g2, w_r, b_r, tiles, n_tiles, off, n_groups, n_experts)

    counts = cnt[0, n_groups:n_groups + n_all]
    padded = ((counts + tme - 1) // tme) * tme
    ends = jnp.cumsum(padded)
    offs = ends - padded
    eid = jnp.arange(n_all, dtype=jnp.int32)
    row0 = jnp.sum(jnp.where(ei[:, :top_k, None] == eid, offs.astype(jnp.int32), 0), axis=-1)
    pos = (row0 + ei[:, top_k:2 * top_k]).T.reshape(-1).astype(jnp.int32)
    n_tile_max = (top_k * n_tok + n_all * (tme - 1)) // tme + 1
    n_rows = n_tile_max * tme
    n_active = ends[-1] // tme
    last = jnp.maximum(n_active - 1, 0)
    tidx = jnp.arange(n_tile_max, dtype=jnp.int32)
    active = tidx < n_active
    tile_x = jnp.where(active, tidx, last).astype(jnp.int32)
    tile_e = jnp.sum((ends[None, :] <= (tile_x * tme)[:, None]).astype(jnp.int32), axis=1)
    tile_e = jnp.minimum(tile_e, n_all - 1)
    prev_e = jnp.concatenate([jnp.full((1,), -1, jnp.int32), tile_e[:-1]])
    first = jnp.logical_and(active, tile_e != prev_e)
    parity = (jnp.cumsum(first.astype(jnp.int32)) + 1) % 2
    flags = (active.astype(jnp.int32) * _ACTIVE + first.astype(jnp.int32) * _FIRST
             + parity * _SLOT).astype(jnp.int32)
    later = jnp.logical_and(eid[None, :] > eid[:, None], (padded > 0)[None, :])
    next_e = jnp.min(jnp.where(later, eid[None, :], n_all), axis=1)
    next_e = jnp.where(next_e < n_all, next_e + layer * n_all, -1).astype(jnp.int32)
    tile_next = jnp.sum(jnp.where(tile_e[:, None] == eid[None, :], next_e[None, :], 0), axis=1)
    tile_w = (tile_e + layer * n_all).astype(jnp.int32)

    next_tile_e = jnp.concatenate([tile_e[1:], jnp.full((1,), -1, jnp.int32)])
    last_of_expert = jnp.logical_or(tile_e != next_tile_e, tidx + 1 >= n_active)
    fill = jnp.logical_or(jnp.logical_not(active), last_of_expert).astype(jnp.int32)

    xs = _dispatch(pos, fill, mx, n_rows, tm, top_k, tme)
    ys = _expert_ffn(tile_w, tile_next.astype(jnp.int32), tile_x, flags, xs,
                     w_gate.reshape(-1, d, w_gate.shape[-1]), w_up.reshape(-1, d, w_up.shape[-1]),
                     w_down.reshape(-1, w_down.shape[-2], d), tme)
    return _combine(pos, ys, h, modl, wf, tiles, n_tiles, off, top_k)


def _rope_tables(n_ctx, n_lat, head_dim):
    rows = n_lat // GRID_W
    row_id = jnp.repeat(jnp.arange(rows, dtype=F32), GRID_W)
    col_id = jnp.tile(jnp.arange(GRID_W, dtype=F32), rows)
    half = head_dim // 2
    inv_freq = ROPE_THETA ** (-jnp.arange(0, half, 2, dtype=F32) / half)
    ang_r = row_id[:, None] * inv_freq[None, :]
    ang_c = col_id[:, None] * inv_freq[None, :]
    ang = jnp.concatenate([ang_r, ang_r, ang_c, ang_c], axis=-1)
    reps = HEAD_LANES // head_dim
    cos = jnp.tile(jnp.cos(ang), (1, reps))
    sin = jnp.tile(jnp.sin(ang), (1, reps))
    quarter = head_dim // 4
    first = (jnp.arange(HEAD_LANES) % (2 * quarter)) < quarter
    s_dn = jnp.where(first[None, :], -sin, 0.0)
    s_up = jnp.where(first[None, :], 0.0, sin)
    pad = lambda t, fill: jnp.concatenate([jnp.full((n_ctx, HEAD_LANES), fill, F32), t], axis=0)
    return pad(cos, 1.0), pad(s_dn, 0.0), pad(s_up, 0.0)


def kernel(x, c, ctx, c_ctx, ada_w, ada_b, norm1_g, norm2_g, gmlp_w_in, gmlp_v_gain, gmlp_w_s, gmlp_b_s, gmlp_w_out, gqa_w_qkv, gqa_q_gain, gqa_k_gain, gqa_w_o, diff_w_qkv, diff_q_gain, diff_k_gain, diff_lam_q1, diff_lam_k1, diff_lam_q2, diff_lam_k2, diff_sub_gain, diff_w_o, moe_w_grp, moe_b_grp, moe_w_exp, moe_b_exp, moe_w_gate, moe_w_up, moe_w_down):
    n_batch, n_lat, d = x.shape
    n_ctx = ctx.shape[1]
    depth = ada_w.shape[0]
    tm = 256 if (n_ctx % 256 == 0 and n_lat % 256 == 0) else 128
    assert n_ctx % tm == 0 and n_lat % tm == 0 and tm % CHUNK == 0 and d % LANES == 0
    ctx_tiles, lat_tiles = n_ctx // tm, n_lat // tm

    mod_rows = -(-(n_batch + 1) // 8) * 8
    cc = jnp.zeros((mod_rows, d), F32).at[:n_batch].set(c).at[n_batch].set(c_ctx)
    mod = _ada_table(cc, ada_w, ada_b)
    mod = mod.reshape(depth, mod_rows, 6, 1, d).transpose(0, 2, 1, 3, 4)

    gqa_heads = gqa_w_o.shape[1] // HEAD_LANES
    gqa_kv_heads = (gqa_w_qkv.shape[2] // HEAD_LANES - gqa_heads) // 2
    diff_heads = diff_w_o.shape[1] // HEAD_LANES
    rope_b = _rope_tables(n_ctx, n_lat, HEAD_LANES)
    rope_c = _rope_tables(n_ctx, n_lat, HEAD_LANES // 2)

    h = jnp.concatenate([ctx, x], axis=1)
    has_ctx = True
    for i in range(depth):
        kind, j = i % N_MIXERS, i // N_MIXERS
        need_ctx = any((k % N_MIXERS) != 0 for k in range(i + 1, depth))
        modl = mod[i]
        g1 = norm1_g[i].reshape(1, d)
        g2 = norm2_g[i].reshape(1, d)
        in_tiles = _Tiles(n_batch, tm, ctx_tiles if has_ctx else 0)
        n_in = (ctx_tiles if has_ctx else 0) + lat_tiles
        keep = has_ctx and need_ctx
        out_tiles = _Tiles(n_batch, tm, ctx_tiles if keep else 0)
        n_out = (ctx_tiles if keep else 0) + lat_tiles
        off = n_in - n_out

        if kind == 0:
            h = _gmlp_layer(h, modl, g1, gmlp_w_in[j].astype(BF16), gmlp_v_gain[j].reshape(1, -1),
                            gmlp_w_s[j].astype(BF16), gmlp_b_s[j].T, gmlp_w_out[j].astype(BF16),
                            out_tiles, n_out, off)
        else:
            if kind == 1:
                q, k, v = _qkv_proj(
                    h, modl, g1, gqa_w_qkv[j].astype(BF16), gqa_q_gain[j].reshape(1, -1),
                    gqa_k_gain[j].reshape(1, -1), rope_b if has_ctx else tuple(t[n_ctx:] for t in rope_b),
                    in_tiles, n_in, n_q=gqa_heads, n_k=gqa_kv_heads, n_v=gqa_kv_heads, halves=1,
                    quarter=HEAD_LANES // 4, q_scale=LOG2E / math.sqrt(HEAD_LANES))
                o = _gqa_attention(q, k, v, tm, n_ctx if has_ctx else 0, n_out, off)
                w_o = gqa_w_o[j]
            else:
                lam_init = 0.8 - 0.6 * math.exp(-0.3 * i)
                q, k, v = _qkv_proj(
                    h, modl, g1, diff_w_qkv[j].astype(BF16), diff_q_gain[j].reshape(1, -1),
                    diff_k_gain[j].reshape(1, -1), rope_c if has_ctx else tuple(t[n_ctx:] for t in rope_c),
                    in_tiles, n_in, n_q=diff_heads, n_k=diff_heads, n_v=diff_heads, halves=2,
                    quarter=HEAD_LANES // 8, q_scale=LOG2E / math.sqrt(HEAD_LANES // 2))
                lam_vecs = [t[j].reshape(1, -1) for t in (diff_lam_q1, diff_lam_k1, diff_lam_q2, diff_lam_k2)]
                o = _diff_attention(q, k, v, lam_vecs, diff_sub_gain[j].reshape(1, -1), lam_init, tm,
                                    n_ctx if has_ctx else 0, n_out, off)
                w_o = diff_w_o[j]
            h = _out_proj(o, w_o.astype(BF16), h, modl, out_tiles, n_out, off)

        h = _moe_layer(i, h, modl, g2, moe_w_grp[i], moe_b_grp[i], moe_w_exp[i], moe_b_exp[i],
                       moe_w_gate, moe_w_up, moe_w_down, out_tiles, n_out, 0)
        has_ctx = keep
    return h[:, n_ctx:] if has_ctx else h
```

```python
import functools
import math

import jax
import jax.numpy as jnp
from jax import lax
from jax.experimental import pallas as pl
from jax.experimental.pallas import tpu as pltpu

EPS = 1e-6
GRID_W = 64
ROPE_THETA = 10000.0
N_MIXERS = 3
CHUNK = 128
LANES = 128
HEAD_LANES = 128
MXU_COLS = 256
MOE_ROW_TILE = 128
DIFF_HEADS_PER_STEP = 4
GQA_KV_HEADS_PER_STEP = 4
VMEM_LIMIT_BYTES = 60000 * 1024
ADA_COL_TILE = 1024
LOG2E = math.log2(math.e)

F32 = jnp.float32
BF16 = jnp.bfloat16


def _cparams(*sem):
    return pltpu.CompilerParams(dimension_semantics=sem, vmem_limit_bytes=VMEM_LIMIT_BYTES)


def _resident(shape):
    nd = len(shape)
    return pl.BlockSpec(shape, lambda *_: (0,) * nd, pipeline_mode=pl.Buffered(1))


def _full(shape):
    nd = len(shape)
    return pl.BlockSpec(shape, lambda *_: (0,) * nd)


def _norm_mod(x, g, shift, scale):
    y = x * lax.rsqrt(jnp.mean(x * x, axis=-1, keepdims=True) + EPS)
    return (y * g) * (1.0 + scale) + shift


def _dot(a, b):
    return jnp.dot(a, b, preferred_element_type=F32)


def _split_bf16(x):
    hi = x.astype(BF16)
    return hi, (x - hi.astype(F32)).astype(BF16)


def _dot_f32(a, b):
    a_hi, a_lo = _split_bf16(a)
    b_hi, b_lo = _split_bf16(b)
    rows = a.shape[0]
    top = _dot(jnp.concatenate([a_hi, a_lo], axis=0), b_hi)
    return top[:rows] + top[rows:] + _dot(a_hi, b_lo)


def _ada_kernel(cc_ref, w_ref, b_ref, o_ref):
    s = jax.nn.silu(cc_ref[...])
    o_ref[...] = _dot_f32(s, w_ref[...]) + b_ref[...]


def _ada_table(cc, ada_w, ada_b):
    depth, d, n6 = ada_w.shape
    rows = cc.shape[0]
    tn = ADA_COL_TILE
    return pl.pallas_call(
        _ada_kernel,
        grid=(depth, n6 // tn),
        in_specs=[
            pl.BlockSpec((rows, d), lambda l, n: (0, 0)),
            pl.BlockSpec((None, d, tn), lambda l, n: (l, 0, n)),
            pl.BlockSpec((None, 1, tn), lambda l, n: (l, 0, n)),
        ],
        out_specs=pl.BlockSpec((None, rows, tn), lambda l, n: (l, 0, n)),
        out_shape=jax.ShapeDtypeStruct((depth, rows, n6), F32),
        compiler_params=_cparams("arbitrary", "arbitrary"),
        name="ada_table",
    )(cc, ada_w, ada_b.reshape(depth, 1, n6))


class _Tiles:
    def __init__(self, n_batch, tm, ctx_tiles):
        self.n_batch, self.tm, self.ctx_tiles = n_batch, tm, ctx_tiles

    def mod(self, j, d):
        nb, ct = self.n_batch, self.ctx_tiles
        return pl.BlockSpec((None, None, 1, d), lambda b, t: (j, jnp.where(t < ct, nb, b), 0, 0))

    def rows(self, d, off=0):
        tm = self.tm
        return pl.BlockSpec((None, tm, d), lambda b, t: (b, t + off, 0))


def _gmlp_kernel(*refs, width, groups, ctx_tiles, split):
    if split:
        c_ref, l_ref = refs[:2]
        x = jnp.where(pl.program_id(1) < ctx_tiles, c_ref[...], l_ref[...])
        refs = refs[2:]
    else:
        x = refs[0][...]
        refs = refs[1:]
    sh_ref, sc_ref, gt_ref, g_ref, win_ref, vg_ref, ws_ref, bs_ref, wout_ref, o_ref, gated_ref = refs
    tm = x.shape[0]
    a = _norm_mod(x, g_ref[...], sh_ref[...], sc_ref[...]).astype(BF16)
    v = jax.nn.gelu(_dot(a, win_ref[:, width:]))
    v = v * lax.rsqrt(jnp.mean(v * v, axis=-1, keepdims=True) + EPS) * vg_ref[...]
    v = v.astype(BF16)
    gd = width // groups
    per = MXU_COLS // gd
    for g0 in range(0, groups, per):
        u = jax.nn.gelu(_dot(a, win_ref[:, g0 * gd:(g0 + per) * gd]))
        for g in range(g0, g0 + per):
            cols = slice(g * gd, (g + 1) * gd)
            ug = u[:, (g - g0) * gd:(g - g0 + 1) * gd]
            for c in range(tm // CHUNK):
                rows = slice(c * CHUNK, (c + 1) * CHUNK)
                s = _dot(ws_ref[g], v[rows, cols]) + bs_ref[:, g:g + 1]
                gated_ref[rows, cols] = (ug[rows] * s).astype(BF16)
    y = _dot(gated_ref[...], wout_ref[...])
    o_ref[...] = x + gt_ref[...] * y


def _gmlp_layer(srcs, modl, g1, w_in, v_gain, w_s, b_s, w_out, tiles, n_tiles, off):
    n_batch, _, d = srcs[-1].shape
    tm = tiles.tm
    width = w_out.shape[0]
    groups = w_s.shape[0]
    split = len(srcs) == 2
    ct = tiles.ctx_tiles
    kern = functools.partial(_gmlp_kernel, width=width, groups=groups, ctx_tiles=ct, split=split)
    if split:
        src_specs = [pl.BlockSpec((None, tm, d), lambda b, t: (b, jnp.minimum(t, ct - 1), 0)),
                     pl.BlockSpec((None, tm, d), lambda b, t: (b, jnp.maximum(t - ct, 0), 0))]
    else:
        src_specs = [tiles.rows(d, off)]
    return pl.pallas_call(
        kern,
        grid=(n_batch, n_tiles),
        in_specs=src_specs + [
            tiles.mod(0, d), tiles.mod(1, d), tiles.mod(2, d),
            _full((1, d)), _resident(w_in.shape), _full((1, width)), _resident(w_s.shape),
            _full(b_s.shape), _resident(w_out.shape),
        ],
        out_specs=tiles.rows(d),
        out_shape=jax.ShapeDtypeStruct((n_batch, n_tiles * tm, d), F32),
        scratch_shapes=[pltpu.VMEM((tm, width), BF16)],
        compiler_params=_cparams("arbitrary", "arbitrary"),
        name="gmlp_mixer",
    )(*srcs, modl, modl, modl, g1, w_in, v_gain, w_s, b_s, w_out)


def _rope(x, cos, s_dn, s_up, quarter):
    return (x * cos + pltpu.roll(x, LANES - quarter, 1) * s_dn + pltpu.roll(x, quarter, 1) * s_up)


def _qkv_kernel(h_ref, sh_ref, sc_ref, g_ref, w_ref, qg_ref, kg_ref, cos_ref, sdn_ref, sup_ref,
                q_ref, k_ref, v_ref, *, n_q, n_k, n_v, halves, quarter, q_scale):
    x = h_ref[...]
    a = _norm_mod(x, g_ref[...], sh_ref[...], sc_ref[...]).astype(BF16)
    cos, sdn, sup = cos_ref[...], sdn_ref[...], sup_ref[...]
    lane = lax.broadcasted_iota(jnp.int32, (1, HEAD_LANES), 1)
    lo = lane < (HEAD_LANES // 2)

    def head_norm(t, gain):
        sq = t * t
        if halves == 1:
            inv = lax.rsqrt(jnp.mean(sq, axis=-1, keepdims=True) + EPS)
        else:
            tot = jnp.sum(sq, axis=-1, keepdims=True)
            s_lo = jnp.sum(jnp.where(lo, sq, 0.0), axis=-1, keepdims=True)
            half = HEAD_LANES // 2
            inv = jnp.where(lo, lax.rsqrt(s_lo / half + EPS), lax.rsqrt((tot - s_lo) / half + EPS))
        return t * inv * gain

    qg, kg = qg_ref[...], kg_ref[...]
    per = MXU_COLS // HEAD_LANES
    for j0 in range(0, n_q + n_k + n_v, per):
        t2 = _dot(a, w_ref[:, j0 * HEAD_LANES:(j0 + per) * HEAD_LANES])
        for j in range(j0, j0 + per):
            t = t2[:, (j - j0) * HEAD_LANES:(j - j0 + 1) * HEAD_LANES]
            if j < n_q:
                t = _rope(head_norm(t, qg), cos, sdn, sup, quarter) * q_scale
                q_ref[j] = t.astype(BF16)
            elif j < n_q + n_k:
                k_ref[j - n_q] = _rope(head_norm(t, kg), cos, sdn, sup, quarter).astype(BF16)
            else:
                v_ref[j - n_q - n_k] = t.astype(BF16)


def _qkv_proj(h, modl, g1, w, q_gain, k_gain, rope_tabs, tiles, n_tiles, *, n_q, n_k, n_v, halves,
              quarter, q_scale):
    n_batch, n_rows, d = h.shape
    tm = tiles.tm
    cos, sdn, sup = rope_tabs
    kern = functools.partial(_qkv_kernel, n_q=n_q, n_k=n_k, n_v=n_v, halves=halves,
                             quarter=quarter, q_scale=q_scale)
    tab = pl.BlockSpec((tm, HEAD_LANES), lambda b, t: (t, 0))

    def heads(n):
        return pl.BlockSpec((None, n, tm, HEAD_LANES), lambda b, t: (b, 0, t, 0))

    def out(n):
        return jax.ShapeDtypeStruct((n_batch, n, n_rows, HEAD_LANES), BF16)

    return pl.pallas_call(
        kern,
        grid=(n_batch, n_tiles),
        in_specs=[
            tiles.rows(d), tiles.mod(0, d), tiles.mod(1, d), _full((1, d)), _resident(w.shape),
            _full((1, HEAD_LANES)), _full((1, HEAD_LANES)), tab, tab, tab,
        ],
        out_specs=[heads(n_q), heads(n_k), heads(n_v)],
        out_shape=[out(n_q), out(n_k), out(n_v)],
        compiler_params=_cparams("arbitrary", "arbitrary"),
        name="qkv_proj",
    )(h, modl, modl, g1, w, q_gain, k_gain, cos, sdn, sup)


def _softmax_pv(q, k, v):
    s = lax.dot_general(q, k, (((1,), (1,)), ((), ())), preferred_element_type=F32)
    p = jnp.exp2(s - jnp.max(s, axis=-1, keepdims=True))
    o = _dot(p.astype(BF16), v)
    return o / jnp.sum(p, axis=-1, keepdims=True)


def _gqa_attn_kernel(q_ref, k_ref, v_ref, o_ref, *, group, n_ctx, ctx_tiles):
    qi = pl.program_id(2)

    def attend(n_keys):
        for j in range(k_ref.shape[0]):
            k, v = k_ref[j, :n_keys], v_ref[j, :n_keys]
            for g in range(group):
                head = j * group + g
                o = _softmax_pv(q_ref[head], k, v)
                o_ref[:, head * HEAD_LANES:(head + 1) * HEAD_LANES] = o.astype(BF16)

    if ctx_tiles:
        pl.when(qi < ctx_tiles)(lambda: attend(n_ctx))
        pl.when(qi >= ctx_tiles)(lambda: attend(k_ref.shape[1]))
    else:
        attend(k_ref.shape[1])


def _gqa_attention(q, k, v, tq, n_ctx, q_tiles, q_off):
    n_batch, n_heads, n_rows, _ = q.shape
    kv_heads = k.shape[1]
    group = n_heads // kv_heads
    hp = GQA_KV_HEADS_PER_STEP
    assert kv_heads % hp == 0
    ctx_tiles = 0 if q_off else n_ctx // tq
    kern = functools.partial(_gqa_attn_kernel, group=group, n_ctx=n_ctx, ctx_tiles=ctx_tiles)
    kv = pl.BlockSpec((None, hp, n_rows, HEAD_LANES), lambda b, h, t: (b, h, 0, 0))
    return pl.pallas_call(
        kern,
        grid=(n_batch, kv_heads // hp, q_tiles),
        in_specs=[
            pl.BlockSpec((None, hp * group, tq, HEAD_LANES), lambda b, h, t: (b, h, t + q_off, 0)),
            kv, kv,
        ],
        out_specs=pl.BlockSpec((None, tq, hp * group * HEAD_LANES), lambda b, h, t: (b, t, h)),
        out_shape=jax.ShapeDtypeStruct((n_batch, q_tiles * tq, n_heads * HEAD_LANES), BF16),
        compiler_params=_cparams("arbitrary", "arbitrary", "arbitrary"),
        name="gqa_attention",
    )(q, k, v)


def _diff_attn_kernel(q_ref, k_ref, v_ref, lq1_ref, lk1_ref, lq2_ref, lk2_ref, sg_ref, o_ref, *,
                      lam_init, n_ctx, ctx_tiles):
    qi = pl.program_id(2)
    lam = (jnp.exp(jnp.sum(lq1_ref[...] * lk1_ref[...], axis=-1, keepdims=True))
           - jnp.exp(jnp.sum(lq2_ref[...] * lk2_ref[...], axis=-1, keepdims=True)) + lam_init)
    lane = lax.broadcasted_iota(jnp.int32, (1, HEAD_LANES), 1)
    lo = lane < (HEAD_LANES // 2)

    def attend(n_keys):
        dims = (((1,), (1,)), ((), ()))
        for j in range(q_ref.shape[0]):
            q, k, v = q_ref[j], k_ref[j, :n_keys], v_ref[j, :n_keys]
            zero = jnp.zeros_like(q)
            s0 = lax.dot_general(jnp.where(lo, q, zero), k, dims, preferred_element_type=F32)
            s1 = lax.dot_general(jnp.where(lo, zero, q), k, dims, preferred_element_type=F32)
            e0 = jnp.exp2(s0 - jnp.max(s0, axis=-1, keepdims=True))
            e1 = jnp.exp2(s1 - jnp.max(s1, axis=-1, keepdims=True))
            c0 = 1.0 / jnp.sum(e0, axis=-1, keepdims=True)
            c1 = lam / jnp.sum(e1, axis=-1, keepdims=True)
            o = _dot(e0.astype(BF16), v) * c0 - _dot(e1.astype(BF16), v) * c1
            o = o * lax.rsqrt(jnp.mean(o * o, axis=-1, keepdims=True) + EPS) * sg_ref[...]
            o_ref[:, j * HEAD_LANES:(j + 1) * HEAD_LANES] = (o * (1.0 - lam_init)).astype(BF16)

    if ctx_tiles:
        pl.when(qi < ctx_tiles)(lambda: attend(n_ctx))
        pl.when(qi >= ctx_tiles)(lambda: attend(k_ref.shape[1]))
    else:
        attend(k_ref.shape[1])


def _diff_attention(q, k, v, lam_vecs, sub_gain, lam_init, tq, n_ctx, q_tiles, q_off):
    n_batch, n_heads, n_rows, _ = q.shape
    ctx_tiles = 0 if q_off else n_ctx // tq
    hp = DIFF_HEADS_PER_STEP
    assert n_heads % hp == 0
    kern = functools.partial(_diff_attn_kernel, lam_init=lam_init, n_ctx=n_ctx, ctx_tiles=ctx_tiles)
    kv = pl.BlockSpec((None, hp, n_rows, HEAD_LANES), lambda b, h, t: (b, h, 0, 0))
    vec = _full(lam_vecs[0].shape)
    return pl.pallas_call(
        kern,
        grid=(n_batch, n_heads // hp, q_tiles),
        in_specs=[
            pl.BlockSpec((None, hp, tq, HEAD_LANES), lambda b, h, t: (b, h, t + q_off, 0)),
            kv, kv, vec, vec, vec, vec, _full((1, HEAD_LANES)),
        ],
        out_specs=pl.BlockSpec((None, tq, hp * HEAD_LANES), lambda b, h, t: (b, t, h)),
        out_shape=jax.ShapeDtypeStruct((n_batch, q_tiles * tq, n_heads * HEAD_LANES), BF16),
        compiler_params=_cparams("arbitrary", "arbitrary", "arbitrary"),
        name="diff_attention",
    )(q, k, v, *lam_vecs, sub_gain)


def _oproj_kernel(o_ref, w_ref, h_ref, gt_ref, out_ref):
    out_ref[...] = h_ref[...] + gt_ref[...] * _dot(o_ref[...], w_ref[...])


def _out_proj(o, w, h, modl, tiles, n_tiles, h_off):
    n_batch, _, d = h.shape
    tm = tiles.tm
    return pl.pallas_call(
        _oproj_kernel,
        grid=(n_batch, n_tiles),
        in_specs=[tiles.rows(o.shape[-1]), _resident(w.shape), tiles.rows(d, h_off), tiles.mod(2, d)],
        out_specs=tiles.rows(d),
        out_shape=jax.ShapeDtypeStruct((n_batch, n_tiles * tm, d), F32),
        compiler_params=_cparams("arbitrary", "arbitrary"),
        name="out_proj",
    )(o, w, h, modl)


def _router_kernel(h_ref, sh_ref, sc_ref, g_ref, wr_ref, br_ref, mx_ref, ei_ref, wf_ref, cnt_ref,
                   carry_ref, *, n_groups, n_experts):
    first = jnp.logical_and(pl.program_id(0) == 0, pl.program_id(1) == 0)

    @pl.when(first)
    def _():
        carry_ref[...] = jnp.zeros_like(carry_ref)

    x = h_ref[...]
    tm = x.shape[0]
    mx = _norm_mod(x, g_ref[...], sh_ref[...], sc_ref[...])
    mx_ref[...] = mx
    logits = _dot_f32(mx, wr_ref[...]) + br_ref[...]
    lane = lax.broadcasted_iota(jnp.int32, logits.shape, 1).astype(F32)
    neg = jnp.float32(-jnp.inf)
    big = jnp.float32(LANES)

    def first_argmax(vals):
        m = jnp.max(vals, axis=-1, keepdims=True)
        return m, jnp.min(jnp.where(vals == m, lane, big), axis=-1, keepdims=True)

    glog = jnp.where(lane < n_groups, logits, neg)
    gmax, gidx = first_argmax(glog)
    grp_w = 1.0 / jnp.sum(jnp.exp(glog - gmax), axis=-1, keepdims=True)
    e_lo = n_groups + gidx * n_experts
    elog = jnp.where(jnp.logical_and(lane >= e_lo, lane < e_lo + n_experts), logits, neg)
    v1, i1 = first_argmax(elog)
    v2, i2 = first_argmax(jnp.where(lane == i1, neg, elog))
    t = jnp.exp(v2 - v1)
    w1 = grp_w / (1.0 + t)
    w2 = grp_w * t / (1.0 + t)

    onehot = jnp.logical_or(lane == i1, lane == i2)
    r_io = lax.broadcasted_iota(jnp.int32, (tm, tm), 0)
    c_io = lax.broadcasted_iota(jnp.int32, (tm, tm), 1)
    tri = jnp.where(r_io >= c_io, 1.0, 0.0).astype(BF16)
    incl = _dot(tri, jnp.where(onehot, 1.0, 0.0).astype(BF16))
    before = carry_ref[...] + incl - 1.0
    r1 = jnp.sum(jnp.where(lane == i1, before, 0.0), axis=-1, keepdims=True)
    r2 = jnp.sum(jnp.where(lane == i2, before, 0.0), axis=-1, keepdims=True)
    carry = carry_ref[...] + incl[tm - 1:tm, :]
    carry_ref[...] = carry
    cnt_ref[...] = carry.astype(jnp.int32)

    ei = jnp.where(lane == 0, i1 - n_groups,
                   jnp.where(lane == 1, i2 - n_groups,
                             jnp.where(lane == 2, r1, jnp.where(lane == 3, r2, 0.0))))
    ei_ref[...] = ei.astype(jnp.int32)
    wf_ref[...] = jnp.where(lane == 0, w1, jnp.where(lane == 1, w2, 0.0))


def _norm_router(h, modl, g2, w_r, b_r, tiles, n_tiles, off, n_groups, n_experts):
    n_batch, _, d = h.shape
    tm = tiles.tm
    n_tok = n_batch * n_tiles * tm
    kern = functools.partial(_router_kernel, n_groups=n_groups, n_experts=n_experts)

    def flat(width):
        return pl.BlockSpec((tm, width), lambda b, t: (b * n_tiles + t, 0))

    return pl.pallas_call(
        kern,
        grid=(n_batch, n_tiles),
        in_specs=[tiles.rows(d, off), tiles.mod(3, d), tiles.mod(4, d), _full((1, d)),
                  _full(w_r.shape), _full((1, LANES))],
        out_specs=[flat(d), flat(LANES), flat(LANES), _full((1, LANES))],
        out_shape=[
            jax.ShapeDtypeStruct((n_tok, d), F32),
            jax.ShapeDtypeStruct((n_tok, LANES), jnp.int32),
            jax.ShapeDtypeStruct((n_tok, LANES), F32),
            jax.ShapeDtypeStruct((1, LANES), jnp.int32),
        ],
        scratch_shapes=[pltpu.VMEM((1, LANES), F32)],
        compiler_params=_cparams("arbitrary", "arbitrary"),
        name="moe_norm_router",
    )(h, modl, modl, g2, w_r, b_r)


SUBLANES = 8


def _dispatch_kernel(pos_ref, fill_ref, mx_ref, xs_ref, zero_ref, sem, fill_sem, *, n_tok, top_k, tme):
    tm = mx_ref.shape[0] * SUBLANES
    i = pl.program_id(0)
    base = i * tm

    @pl.when(i == 0)
    def _():
        zero_ref[...] = jnp.zeros_like(zero_ref)

        def fill(wait):
            def body(j, carry):
                @pl.when(fill_ref[j] != 0)
                def _():
                    dst = xs_ref.at[pl.ds(pl.multiple_of(j * tme, tme), tme), :]
                    cp = pltpu.make_async_copy(zero_ref, dst, fill_sem)
                    cp.wait() if wait else cp.start()
                return carry
            lax.fori_loop(0, fill_ref.shape[0], body, 0)

        fill(False)
        fill(True)

    def rows(g, wait):
        for u in range(SUBLANES):
            for kk in range(top_k):
                dst = 0 if wait else pos_ref[kk * n_tok + base + g * SUBLANES + u]
                cp = pltpu.make_async_copy(mx_ref.at[g, pl.ds(u, 1), :],
                                           xs_ref.at[pl.ds(dst, 1), :], sem)
                cp.wait() if wait else cp.start(priority=kk % 2)

    def start(g, carry):
        rows(g, False)
        return carry

    def wait(g, carry):
        rows(0, True)
        return carry

    lax.fori_loop(0, tm // SUBLANES, start, 0)
    lax.fori_loop(0, tm // SUBLANES, wait, 0)


def _dispatch(pos, fill, mx, n_rows, tm, top_k, tme):
    n_tok, d = mx.shape
    kern = functools.partial(_dispatch_kernel, n_tok=n_tok, top_k=top_k, tme=tme)
    return pl.pallas_call(
        kern,
        grid_spec=pltpu.PrefetchScalarGridSpec(
            num_scalar_prefetch=2,
            grid=(n_tok // tm,),
            in_specs=[pl.BlockSpec((tm // SUBLANES, SUBLANES, d), lambda i, pos, fill: (i, 0, 0))],
            out_specs=pl.BlockSpec(memory_space=pl.ANY),
            scratch_shapes=[pltpu.VMEM((tme, d), mx.dtype), pltpu.SemaphoreType.DMA(()),
                            pltpu.SemaphoreType.DMA(())],
        ),
        out_shape=jax.ShapeDtypeStruct((n_rows, d), mx.dtype),
        compiler_params=_cparams("arbitrary"),
        name="moe_dispatch",
    )(pos, fill, mx.reshape(n_tok // SUBLANES, SUBLANES, d))


_ACTIVE, _FIRST, _SLOT = 1, 2, 4


def _ffn_kernel(te_ref, tn_ref, tx_ref, fl_ref, xs_ref, wg_hbm, wu_hbm, wd_hbm, ys_ref,
                wg_f, wu_f, wd_f, wg_b, wu_b, wd_b, sems):
    del tx_ref
    i = pl.program_id(0)
    flags = fl_ref[i]

    slot = jnp.where((flags & _SLOT) != 0, 1, 0)

    def weight_copies(e, s):
        return (pltpu.make_async_copy(wg_hbm.at[e], wg_f.at[s], sems.at[s, 0]),
                pltpu.make_async_copy(wu_hbm.at[e], wu_f.at[s], sems.at[s, 1]),
                pltpu.make_async_copy(wd_hbm.at[e], wd_f.at[s], sems.at[s, 2]))

    def start_weights(e, s):
        for cp, queue in zip(weight_copies(e, s), (1, 1, 0)):
            cp.start(priority=queue)

    @pl.when(jnp.logical_and(i == 0, (flags & _FIRST) != 0))
    def _():
        start_weights(te_ref[0], slot)

    @pl.when((flags & _FIRST) != 0)
    def _():
        for cp in weight_copies(te_ref[i], slot):
            cp.wait()

        @pl.when(tn_ref[i] >= 0)
        def _():
            start_weights(tn_ref[i], 1 - slot)

        wg_b[...] = wg_f[slot].astype(BF16)
        wu_b[...] = wu_f[slot].astype(BF16)
        wd_b[...] = wd_f[slot].astype(BF16)

    @pl.when((flags & _ACTIVE) != 0)
    def _():
        x = xs_ref[...].astype(BF16)
        a = _dot(x, wg_b[...])
        b = _dot(x, wu_b[...])
        ys_ref[...] = _dot((jax.nn.silu(a) * b).astype(BF16), wd_b[...])

    @pl.when((flags & _ACTIVE) == 0)
    def _():
        ys_ref[...] = jnp.zeros_like(ys_ref)


def _expert_ffn(tile_e, tile_next, tile_x, flags, xs, w_gate, w_up, w_down, tme):
    n_rows, d = xs.shape
    n_tiles = n_rows // tme
    f = w_gate.shape[-1]
    any_spec = pl.BlockSpec(memory_space=pl.ANY)
    return pl.pallas_call(
        _ffn_kernel,
        grid_spec=pltpu.PrefetchScalarGridSpec(
            num_scalar_prefetch=4,
            grid=(n_tiles,),
            in_specs=[pl.BlockSpec((tme, d), lambda i, te, tn, tx, fl: (tx[i], 0)),
                      any_spec, any_spec, any_spec],
            out_specs=pl.BlockSpec((tme, d), lambda i, te, tn, tx, fl: (i, 0)),
            scratch_shapes=[pltpu.VMEM((2, d, f), F32), pltpu.VMEM((2, d, f), F32),
                            pltpu.VMEM((2, f, d), F32),
                            pltpu.VMEM((d, f), BF16), pltpu.VMEM((d, f), BF16),
                            pltpu.VMEM((f, d), BF16), pltpu.SemaphoreType.DMA((2, 3))],
        ),
        out_shape=jax.ShapeDtypeStruct((n_rows, d), F32),
        compiler_params=_cparams("arbitrary"),
        name="moe_expert_ffn",
    )(tile_e, tile_next, tile_x, flags, xs, w_gate, w_up, w_down)


def _combine_kernel(pos_ref, ys_ref, h_ref, gt_ref, wf_ref, o_ref, buf, sems, *, n_tok, top_k):
    tm = h_ref.shape[0]
    i = pl.program_id(0)
    n = pl.num_programs(0)

    def gather(tile, slot, wait):
        def body(g, carry):
            for u in range(SUBLANES):
                for kk in range(top_k):
                    src = 0 if wait else pos_ref[kk * n_tok + tile * tm + g * SUBLANES + u]
                    cp = pltpu.make_async_copy(ys_ref.at[pl.ds(src, 1), :],
                                               buf.at[slot, kk, 0 if wait else g, pl.ds(u, 1), :],
                                               sems.at[slot])
                    cp.wait() if wait else cp.start(priority=kk % 2)
            return carry
        lax.fori_loop(0, tm // SUBLANES, body, 0)

    @pl.when(i == 0)
    def _():
        gather(0, 0, False)

    @pl.when(i + 1 < n)
    def _():
        gather(i + 1, (i + 1) % 2, False)

    slot = i % 2
    gather(i, slot, True)
    wf = wf_ref[...]
    d = h_ref.shape[1]
    y = buf[slot, 0].reshape(tm, d) * wf[:, 0:1]
    for kk in range(1, top_k):
        y = y + buf[slot, kk].reshape(tm, d) * wf[:, kk:kk + 1]
    o_ref[...] = h_ref[...] + gt_ref[...] * y


def _combine(pos, ys, h, modl, wf, tiles, n_tiles, off, top_k):
    n_batch, _, d = h.shape
    tm = tiles.tm
    n_tok = n_batch * n_tiles * tm
    nb, ct = n_batch, tiles.ctx_tiles
    kern = functools.partial(_combine_kernel, n_tok=n_tok, top_k=top_k)

    def bt(i):
        return i // n_tiles, i % n_tiles

    return pl.pallas_call(
        kern,
        grid_spec=pltpu.PrefetchScalarGridSpec(
            num_scalar_prefetch=1,
            grid=(n_batch * n_tiles,),
            in_specs=[
                pl.BlockSpec(memory_space=pl.ANY),
                pl.BlockSpec((None, tm, d), lambda i, pos: (bt(i)[0], bt(i)[1] + off, 0)),
                pl.BlockSpec((None, None, 1, d),
                             lambda i, pos: (5, jnp.where(bt(i)[1] < ct, nb, bt(i)[0]), 0, 0)),
                pl.BlockSpec((tm, LANES), lambda i, pos: (i, 0)),
            ],
            out_specs=pl.BlockSpec((None, tm, d), lambda i, pos: (bt(i)[0], bt(i)[1], 0)),
            scratch_shapes=[pltpu.VMEM((2, top_k, tm // SUBLANES, SUBLANES, d), F32),
                            pltpu.SemaphoreType.DMA((2,))],
        ),
        out_shape=jax.ShapeDtypeStruct((n_batch, n_tiles * tm, d), F32),
        compiler_params=_cparams("arbitrary"),
        name="moe_combine",
    )(pos, ys, h, modl, wf)


def _moe_layer(layer, h, modl, g2, w_grp, b_grp, w_exp, b_exp, w_gate, w_up, w_down, tiles, n_tiles, off):
    n_batch, _, d = h.shape
    tm = tiles.tm
    n_groups, n_experts = w_gate.shape[1], w_gate.shape[2]
    n_all = n_groups * n_experts
    top_k = 2
    tme = MOE_ROW_TILE
    n_tok = n_batch * n_tiles * tm

    w_r = jnp.zeros((d, LANES), F32).at[:, :n_groups].set(w_grp).at[:, n_groups:n_groups + n_all].set(w_exp)
    b_r = jnp.zeros((1, LANES), F32).at[0, :n_groups].set(b_grp).at[0, n_groups:n_groups + n_all].set(b_exp)
    mx, ei, wf, cnt = _norm_router(h, modl, g2, w_r, b_r, tiles, n_tiles, off, n_groups, n_experts)

    counts = cnt[0, n_groups:n_groups + n_all]
    padded = ((counts + tme - 1) // tme) * tme
    ends = jnp.cumsum(padded)
    offs = ends - padded
    eid = jnp.arange(n_all, dtype=jnp.int32)
    row0 = jnp.sum(jnp.where(ei[:, :top_k, None] == eid, offs.astype(jnp.int32), 0), axis=-1)
    pos = (row0 + ei[:, top_k:2 * top_k]).T.reshape(-1).astype(jnp.int32)
    n_tile_max = (top_k * n_tok + n_all * (tme - 1)) // tme + 1
    n_rows = n_tile_max * tme
    n_active = ends[-1] // tme
    last = jnp.maximum(n_active - 1, 0)
    tidx = jnp.arange(n_tile_max, dtype=jnp.int32)
    active = tidx < n_active
    tile_x = jnp.where(active, tidx, last).astype(jnp.int32)
    tile_e = jnp.sum((ends[None, :] <= (tile_x * tme)[:, None]).astype(jnp.int32), axis=1)
    tile_e = jnp.minimum(tile_e, n_all - 1)
    prev_e = jnp.concatenate([jnp.full((1,), -1, jnp.int32), tile_e[:-1]])
    first = jnp.logical_and(active, tile_e != prev_e)
    parity = (jnp.cumsum(first.astype(jnp.int32)) + 1) % 2
    flags = (active.astype(jnp.int32) * _ACTIVE + first.astype(jnp.int32) * _FIRST
             + parity * _SLOT).astype(jnp.int32)
    later = jnp.logical_and(eid[None, :] > eid[:, None], (padded > 0)[None, :])
    next_e = jnp.min(jnp.where(later, eid[None, :], n_all), axis=1)
    next_e = jnp.where(next_e < n_all, next_e + layer * n_all, -1).astype(jnp.int32)
    tile_next = jnp.sum(jnp.where(tile_e[:, None] == eid[None, :], next_e[None, :], 0), axis=1)
    tile_w = (tile_e + layer * n_all).astype(jnp.int32)

    next_tile_e = jnp.concatenate([tile_e[1:], jnp.full((1,), -1, jnp.int32)])
    last_of_expert = jnp.logical_or(tile_e != next_tile_e, tidx + 1 >= n_active)
    fill = jnp.logical_or(jnp.logical_not(active), last_of_expert).astype(jnp.int32)

    xs = _dispatch(pos, fill, mx, n_rows, tm, top_k, tme)
    ys = _expert_ffn(tile_w, tile_next.astype(jnp.int32), tile_x, flags, xs,
                     w_gate.reshape(-1, d, w_gate.shape[-1]), w_up.reshape(-1, d, w_up.shape[-1]),
                     w_down.reshape(-1, w_down.shape[-2], d), tme)
    return _combine(pos, ys, h, modl, wf, tiles, n_tiles, off, top_k)


def _rope_tables(n_ctx, n_lat, head_dim):
    rows = n_lat // GRID_W
    row_id = jnp.repeat(jnp.arange(rows, dtype=F32), GRID_W)
    col_id = jnp.tile(jnp.arange(GRID_W, dtype=F32), rows)
    half = head_dim // 2
    inv_freq = ROPE_THETA ** (-jnp.arange(0, half, 2, dtype=F32) / half)
    ang_r = row_id[:, None] * inv_freq[None, :]
    ang_c = col_id[:, None] * inv_freq[None, :]
    ang = jnp.concatenate([ang_r, ang_r, ang_c, ang_c], axis=-1)
    reps = HEAD_LANES // head_dim
    cos = jnp.tile(jnp.cos(ang), (1, reps))
    sin = jnp.tile(jnp.sin(ang), (1, reps))
    quarter = head_dim // 4
    first = (jnp.arange(HEAD_LANES) % (2 * quarter)) < quarter
    s_dn = jnp.where(first[None, :], -sin, 0.0)
    s_up = jnp.where(first[None, :], 0.0, sin)
    pad = lambda t, fill: jnp.concatenate([jnp.full((n_ctx, HEAD_LANES), fill, F32), t], axis=0)
    return pad(cos, 1.0), pad(s_dn, 0.0), pad(s_up, 0.0)


def kernel(x, c, ctx, c_ctx, ada_w, ada_b, norm1_g, norm2_g, gmlp_w_in, gmlp_v_gain, gmlp_w_s, gmlp_b_s, gmlp_w_out, gqa_w_qkv, gqa_q_gain, gqa_k_gain, gqa_w_o, diff_w_qkv, diff_q_gain, diff_k_gain, diff_lam_q1, diff_lam_k1, diff_lam_q2, diff_lam_k2, diff_sub_gain, diff_w_o, moe_w_grp, moe_b_grp, moe_w_exp, moe_b_exp, moe_w_gate, moe_w_up, moe_w_down):
    n_batch, n_lat, d = x.shape
    n_ctx = ctx.shape[1]
    depth = ada_w.shape[0]
    tm = 256 if (n_ctx % 256 == 0 and n_lat % 256 == 0) else 128
    assert n_ctx % tm == 0 and n_lat % tm == 0 and tm % CHUNK == 0 and d % LANES == 0
    ctx_tiles, lat_tiles = n_ctx // tm, n_lat // tm

    mod_rows = -(-(n_batch + 1) // 8) * 8
    cc = jnp.zeros((mod_rows, d), F32).at[:n_batch].set(c).at[n_batch].set(c_ctx)
    mod = _ada_table(cc, ada_w, ada_b)
    mod = mod.reshape(depth, mod_rows, 6, 1, d).transpose(0, 2, 1, 3, 4)

    gqa_heads = gqa_w_o.shape[1] // HEAD_LANES
    gqa_kv_heads = (gqa_w_qkv.shape[2] // HEAD_LANES - gqa_heads) // 2
    diff_heads = diff_w_o.shape[1] // HEAD_LANES
    rope_b = _rope_tables(n_ctx, n_lat, HEAD_LANES)
    rope_c = _rope_tables(n_ctx, n_lat, HEAD_LANES // 2)

    h = None
    has_ctx = True
    for i in range(depth):
        kind, j = i % N_MIXERS, i // N_MIXERS
        need_ctx = any((k % N_MIXERS) != 0 for k in range(i + 1, depth))
        modl = mod[i]
        g1 = norm1_g[i].reshape(1, d)
        g2 = norm2_g[i].reshape(1, d)
        in_tiles = _Tiles(n_batch, tm, ctx_tiles if has_ctx else 0)
        n_in = (ctx_tiles if has_ctx else 0) + lat_tiles
        keep = has_ctx and need_ctx
        out_tiles = _Tiles(n_batch, tm, ctx_tiles if keep else 0)
        n_out = (ctx_tiles if keep else 0) + lat_tiles
        off = n_in - n_out

        if kind == 0:
            srcs = (ctx, x) if h is None else (h,)
            h = _gmlp_layer(srcs, modl, g1, gmlp_w_in[j].astype(BF16), gmlp_v_gain[j].reshape(1, -1),
                            gmlp_w_s[j].astype(BF16), gmlp_b_s[j].T, gmlp_w_out[j].astype(BF16),
                            out_tiles, n_out, off)
        else:
            if h is None:
                h = jnp.concatenate([ctx, x], axis=1)
            if kind == 1:
                q, k, v = _qkv_proj(
                    h, modl, g1, gqa_w_qkv[j].astype(BF16), gqa_q_gain[j].reshape(1, -1),
                    gqa_k_gain[j].reshape(1, -1), rope_b if has_ctx else tuple(t[n_ctx:] for t in rope_b),
                    in_tiles, n_in, n_q=gqa_heads, n_k=gqa_kv_heads, n_v=gqa_kv_heads, halves=1,
                    quarter=HEAD_LANES // 4, q_scale=LOG2E / math.sqrt(HEAD_LANES))
                o = _gqa_attention(q, k, v, tm, n_ctx if has_ctx else 0, n_out, off)
                w_o = gqa_w_o[j]
            else:
                lam_init = 0.8 - 0.6 * math.exp(-0.3 * i)
                q, k, v = _qkv_proj(
                    h, modl, g1, diff_w_qkv[j].astype(BF16), diff_q_gain[j].reshape(1, -1),
                    diff_k_gain[j].reshape(1, -1), rope_c if has_ctx else tuple(t[n_ctx:] for t in rope_c),
                    in_tiles, n_in, n_q=diff_heads, n_k=diff_heads, n_v=diff_heads, halves=2,
                    quarter=HEAD_LANES // 8, q_scale=LOG2E / math.sqrt(HEAD_LANES // 2))
                lam_vecs = [t[j].reshape(1, -1) for t in (diff_lam_q1, diff_lam_k1, diff_lam_q2, diff_lam_k2)]
                o = _diff_attention(q, k, v, lam_vecs, diff_sub_gain[j].reshape(1, -1), lam_init, tm,
                                    n_ctx if has_ctx else 0, n_out, off)
                w_o = diff_w_o[j]
            h = _out_proj(o, w_o.astype(BF16), h, modl, out_tiles, n_out, off)

        h = _moe_layer(i, h, modl, g2, moe_w_grp[i], moe_b_grp[i], moe_w_exp[i], moe_b_exp[i],
                       moe_w_gate, moe_w_up, moe_w_down, out_tiles, n_out, 0)
        has_ctx = keep
    return h[:, n_ctx:] if has_ctx else h
```

```python
import functools
import math

import jax
import jax.numpy as jnp
from jax import lax
from jax.experimental import pallas as pl
from jax.experimental.pallas import tpu as pltpu

EPS = 1e-6
GRID_W = 64
ROPE_THETA = 10000.0
N_MIXERS = 3
CHUNK = 128
LANES = 128
HEAD_LANES = 128
MXU_COLS = 256
MOE_ROW_TILE = 128
DIFF_HEADS_PER_STEP = 4
GQA_KV_HEADS_PER_STEP = 4
VMEM_LIMIT_BYTES = 60000 * 1024
ADA_COL_TILE = 1024
LOG2E = math.log2(math.e)

F32 = jnp.float32
BF16 = jnp.bfloat16


def _cparams(*sem):
    return pltpu.CompilerParams(dimension_semantics=sem, vmem_limit_bytes=VMEM_LIMIT_BYTES)


def _resident(shape):
    nd = len(shape)
    return pl.BlockSpec(shape, lambda *_: (0,) * nd, pipeline_mode=pl.Buffered(1))


def _full(shape):
    nd = len(shape)
    return pl.BlockSpec(shape, lambda *_: (0,) * nd)


def _norm_mod(x, g, shift, scale):
    y = x * lax.rsqrt(jnp.mean(x * x, axis=-1, keepdims=True) + EPS)
    return (y * g) * (1.0 + scale) + shift


def _dot(a, b):
    return jnp.dot(a, b, preferred_element_type=F32)


def _pack_bf16_pairs(x):
    half = x.shape[1] // 2
    bits = lambda t: lax.bitcast_convert_type(t.astype(BF16).astype(F32), jnp.uint32)
    return (bits(x[:, :half]) >> 16) | (bits(x[:, half:]) & jnp.uint32(0xFFFF0000))


def _unpack_bf16_pairs(w):
    lo = lax.bitcast_convert_type(w << 16, F32)
    hi = lax.bitcast_convert_type(w & jnp.uint32(0xFFFF0000), F32)
    return lo, hi


def _split_bf16(x):
    hi = x.astype(BF16)
    return hi, (x - hi.astype(F32)).astype(BF16)


def _dot_f32(a, b):
    a_hi, a_lo = _split_bf16(a)
    b_hi, b_lo = _split_bf16(b)
    rows = a.shape[0]
    top = _dot(jnp.concatenate([a_hi, a_lo], axis=0), b_hi)
    return top[:rows] + top[rows:] + _dot(a_hi, b_lo)


def _ada_kernel(cc_ref, w_ref, b_ref, o_ref):
    s = jax.nn.silu(cc_ref[...])
    o_ref[...] = _dot_f32(s, w_ref[...]) + b_ref[...]


def _ada_table(cc, ada_w, ada_b):
    depth, d, n6 = ada_w.shape
    rows = cc.shape[0]
    tn = ADA_COL_TILE
    return pl.pallas_call(
        _ada_kernel,
        grid=(depth, n6 // tn),
        in_specs=[
            pl.BlockSpec((rows, d), lambda l, n: (0, 0)),
            pl.BlockSpec((None, d, tn), lambda l, n: (l, 0, n)),
            pl.BlockSpec((None, 1, tn), lambda l, n: (l, 0, n)),
        ],
        out_specs=pl.BlockSpec((None, rows, tn), lambda l, n: (l, 0, n)),
        out_shape=jax.ShapeDtypeStruct((depth, rows, n6), F32),
        compiler_params=_cparams("arbitrary", "arbitrary"),
        name="ada_table",
    )(cc, ada_w, ada_b.reshape(depth, 1, n6))


class _Tiles:
    def __init__(self, n_batch, tm, ctx_tiles):
        self.n_batch, self.tm, self.ctx_tiles = n_batch, tm, ctx_tiles

    def mod(self, j, d):
        nb, ct = self.n_batch, self.ctx_tiles
        return pl.BlockSpec((None, None, 1, d), lambda b, t: (j, jnp.where(t < ct, nb, b), 0, 0))

    def rows(self, d, off=0):
        tm = self.tm
        return pl.BlockSpec((None, tm, d), lambda b, t: (b, t + off, 0))


def _gmlp_kernel(*refs, width, groups, ctx_tiles, split):
    if split:
        c_ref, l_ref = refs[:2]
        x = jnp.where(pl.program_id(1) < ctx_tiles, c_ref[...], l_ref[...])
        refs = refs[2:]
    else:
        x = refs[0][...]
        refs = refs[1:]
    sh_ref, sc_ref, gt_ref, g_ref, win_ref, vg_ref, ws_ref, bs_ref, wout_ref, o_ref, gated_ref = refs
    tm = x.shape[0]
    a = _norm_mod(x, g_ref[...], sh_ref[...], sc_ref[...]).astype(BF16)
    v = jax.nn.gelu(_dot(a, win_ref[:, width:]))
    v = v * lax.rsqrt(jnp.mean(v * v, axis=-1, keepdims=True) + EPS) * vg_ref[...]
    v = v.astype(BF16)
    gd = width // groups
    per = MXU_COLS // gd
    for g0 in range(0, groups, per):
        u = jax.nn.gelu(_dot(a, win_ref[:, g0 * gd:(g0 + per) * gd]))
        for g in range(g0, g0 + per):
            cols = slice(g * gd, (g + 1) * gd)
            ug = u[:, (g - g0) * gd:(g - g0 + 1) * gd]
            for c in range(tm // CHUNK):
                rows = slice(c * CHUNK, (c + 1) * CHUNK)
                s = _dot(ws_ref[g], v[rows, cols]) + bs_ref[:, g:g + 1]
                gated_ref[rows, cols] = (ug[rows] * s).astype(BF16)
    y = _dot(gated_ref[...], wout_ref[...])
    o_ref[...] = x + gt_ref[...] * y


def _gmlp_layer(srcs, modl, g1, w_in, v_gain, w_s, b_s, w_out, tiles, n_tiles, off):
    n_batch, _, d = srcs[-1].shape
    tm = tiles.tm
    width = w_out.shape[0]
    groups = w_s.shape[0]
    split = len(srcs) == 2
    ct = tiles.ctx_tiles
    kern = functools.partial(_gmlp_kernel, width=width, groups=groups, ctx_tiles=ct, split=split)
    if split:
        src_specs = [pl.BlockSpec((None, tm, d), lambda b, t: (b, jnp.minimum(t, ct - 1), 0)),
                     pl.BlockSpec((None, tm, d), lambda b, t: (b, jnp.maximum(t - ct, 0), 0))]
    else:
        src_specs = [tiles.rows(d, off)]
    return pl.pallas_call(
        kern,
        grid=(n_batch, n_tiles),
        in_specs=src_specs + [
            tiles.mod(0, d), tiles.mod(1, d), tiles.mod(2, d),
            _full((1, d)), _resident(w_in.shape), _full((1, width)), _resident(w_s.shape),
            _full(b_s.shape), _resident(w_out.shape),
        ],
        out_specs=tiles.rows(d),
        out_shape=jax.ShapeDtypeStruct((n_batch, n_tiles * tm, d), F32),
        scratch_shapes=[pltpu.VMEM((tm, width), BF16)],
        compiler_params=_cparams("arbitrary", "arbitrary"),
        name="gmlp_mixer",
    )(*srcs, modl, modl, modl, g1, w_in, v_gain, w_s, b_s, w_out)


def _rope(x, cos, s_dn, s_up, quarter):
    return (x * cos + pltpu.roll(x, LANES - quarter, 1) * s_dn + pltpu.roll(x, quarter, 1) * s_up)


def _qkv_kernel(h_ref, sh_ref, sc_ref, g_ref, w_ref, qg_ref, kg_ref, cos_ref, sdn_ref, sup_ref,
                q_ref, k_ref, v_ref, *, n_q, n_k, n_v, halves, quarter, q_scale):
    x = h_ref[...]
    a = _norm_mod(x, g_ref[...], sh_ref[...], sc_ref[...]).astype(BF16)
    cos, sdn, sup = cos_ref[...], sdn_ref[...], sup_ref[...]
    lane = lax.broadcasted_iota(jnp.int32, (1, HEAD_LANES), 1)
    lo = lane < (HEAD_LANES // 2)

    def head_norm(t, gain):
        sq = t * t
        if halves == 1:
            inv = lax.rsqrt(jnp.mean(sq, axis=-1, keepdims=True) + EPS)
        else:
            tot = jnp.sum(sq, axis=-1, keepdims=True)
            s_lo = jnp.sum(jnp.where(lo, sq, 0.0), axis=-1, keepdims=True)
            half = HEAD_LANES // 2
            inv = jnp.where(lo, lax.rsqrt(s_lo / half + EPS), lax.rsqrt((tot - s_lo) / half + EPS))
        return t * inv * gain

    qg, kg = qg_ref[...], kg_ref[...]
    per = MXU_COLS // HEAD_LANES
    for j0 in range(0, n_q + n_k + n_v, per):
        t2 = _dot(a, w_ref[:, j0 * HEAD_LANES:(j0 + per) * HEAD_LANES])
        for j in range(j0, j0 + per):
            t = t2[:, (j - j0) * HEAD_LANES:(j - j0 + 1) * HEAD_LANES]
            if j < n_q:
                t = _rope(head_norm(t, qg), cos, sdn, sup, quarter) * q_scale
                q_ref[j] = t.astype(BF16)
            elif j < n_q + n_k:
                k_ref[j - n_q] = _rope(head_norm(t, kg), cos, sdn, sup, quarter).astype(BF16)
            else:
                v_ref[j - n_q - n_k] = t.astype(BF16)


def _qkv_proj(h, modl, g1, w, q_gain, k_gain, rope_tabs, tiles, n_tiles, *, n_q, n_k, n_v, halves,
              quarter, q_scale):
    n_batch, n_rows, d = h.shape
    tm = tiles.tm
    cos, sdn, sup = rope_tabs
    kern = functools.partial(_qkv_kernel, n_q=n_q, n_k=n_k, n_v=n_v, halves=halves,
                             quarter=quarter, q_scale=q_scale)
    tab = pl.BlockSpec((tm, HEAD_LANES), lambda b, t: (t, 0))

    def heads(n):
        return pl.BlockSpec((None, n, tm, HEAD_LANES), lambda b, t: (b, 0, t, 0))

    def out(n):
        return jax.ShapeDtypeStruct((n_batch, n, n_rows, HEAD_LANES), BF16)

    return pl.pallas_call(
        kern,
        grid=(n_batch, n_tiles),
        in_specs=[
            tiles.rows(d), tiles.mod(0, d), tiles.mod(1, d), _full((1, d)), _resident(w.shape),
            _full((1, HEAD_LANES)), _full((1, HEAD_LANES)), tab, tab, tab,
        ],
        out_specs=[heads(n_q), heads(n_k), heads(n_v)],
        out_shape=[out(n_q), out(n_k), out(n_v)],
        compiler_params=_cparams("arbitrary", "arbitrary"),
        name="qkv_proj",
    )(h, modl, modl, g1, w, q_gain, k_gain, cos, sdn, sup)


def _softmax_pv(q, k, v):
    s = lax.dot_general(q, k, (((1,), (1,)), ((), ())), preferred_element_type=F32)
    p = jnp.exp2(s - jnp.max(s, axis=-1, keepdims=True))
    o = _dot(p.astype(BF16), v)
    return o / jnp.sum(p, axis=-1, keepdims=True)


def _gqa_attn_kernel(q_ref, k_ref, v_ref, o_ref, *, group, n_ctx, ctx_tiles):
    qi = pl.program_id(2)

    def attend(n_keys):
        for j in range(k_ref.shape[0]):
            k, v = k_ref[j, :n_keys], v_ref[j, :n_keys]
            for g in range(group):
                head = j * group + g
                o = _softmax_pv(q_ref[head], k, v)
                o_ref[:, head * HEAD_LANES:(head + 1) * HEAD_LANES] = o.astype(BF16)

    if ctx_tiles:
        pl.when(qi < ctx_tiles)(lambda: attend(n_ctx))
        pl.when(qi >= ctx_tiles)(lambda: attend(k_ref.shape[1]))
    else:
        attend(k_ref.shape[1])


def _gqa_attention(q, k, v, tq, n_ctx, q_tiles, q_off):
    n_batch, n_heads, n_rows, _ = q.shape
    kv_heads = k.shape[1]
    group = n_heads // kv_heads
    hp = GQA_KV_HEADS_PER_STEP
    assert kv_heads % hp == 0
    ctx_tiles = 0 if q_off else n_ctx // tq
    kern = functools.partial(_gqa_attn_kernel, group=group, n_ctx=n_ctx, ctx_tiles=ctx_tiles)
    kv = pl.BlockSpec((None, hp, n_rows, HEAD_LANES), lambda b, h, t: (b, h, 0, 0))
    return pl.pallas_call(
        kern,
        grid=(n_batch, kv_heads // hp, q_tiles),
        in_specs=[
            pl.BlockSpec((None, hp * group, tq, HEAD_LANES), lambda b, h, t: (b, h, t + q_off, 0)),
            kv, kv,
        ],
        out_specs=pl.BlockSpec((None, tq, hp * group * HEAD_LANES), lambda b, h, t: (b, t, h)),
        out_shape=jax.ShapeDtypeStruct((n_batch, q_tiles * tq, n_heads * HEAD_LANES), BF16),
        compiler_params=_cparams("arbitrary", "arbitrary", "arbitrary"),
        name="gqa_attention",
    )(q, k, v)


def _diff_attn_kernel(q_ref, k_ref, v_ref, lq1_ref, lk1_ref, lq2_ref, lk2_ref, sg_ref, o_ref, *,
                      lam_init, n_ctx, ctx_tiles):
    qi = pl.program_id(2)
    lam = (jnp.exp(jnp.sum(lq1_ref[...] * lk1_ref[...], axis=-1, keepdims=True))
           - jnp.exp(jnp.sum(lq2_ref[...] * lk2_ref[...], axis=-1, keepdims=True)) + lam_init)
    lane = lax.broadcasted_iota(jnp.int32, (1, HEAD_LANES), 1)
    lo = lane < (HEAD_LANES // 2)

    def attend(n_keys):
        dims = (((1,), (1,)), ((), ()))
        for j in range(q_ref.shape[0]):
            q, k, v = q_ref[j], k_ref[j, :n_keys], v_ref[j, :n_keys]
            zero = jnp.zeros_like(q)
            s0 = lax.dot_general(jnp.where(lo, q, zero), k, dims, preferred_element_type=F32)
            s1 = lax.dot_general(jnp.where(lo, zero, q), k, dims, preferred_element_type=F32)
            e0 = jnp.exp2(s0 - jnp.max(s0, axis=-1, keepdims=True))
            e1 = jnp.exp2(s1 - jnp.max(s1, axis=-1, keepdims=True))
            c0 = 1.0 / jnp.sum(e0, axis=-1, keepdims=True)
            c1 = lam / jnp.sum(e1, axis=-1, keepdims=True)
            o = _dot(e0.astype(BF16), v) * c0 - _dot(e1.astype(BF16), v) * c1
            o = o * lax.rsqrt(jnp.mean(o * o, axis=-1, keepdims=True) + EPS) * sg_ref[...]
            o_ref[:, j * HEAD_LANES:(j + 1) * HEAD_LANES] = (o * (1.0 - lam_init)).astype(BF16)

    if ctx_tiles:
        pl.when(qi < ctx_tiles)(lambda: attend(n_ctx))
        pl.when(qi >= ctx_tiles)(lambda: attend(k_ref.shape[1]))
    else:
        attend(k_ref.shape[1])


def _diff_attention(q, k, v, lam_vecs, sub_gain, lam_init, tq, n_ctx, q_tiles, q_off):
    n_batch, n_heads, n_rows, _ = q.shape
    ctx_tiles = 0 if q_off else n_ctx // tq
    hp = DIFF_HEADS_PER_STEP
    assert n_heads % hp == 0
    kern = functools.partial(_diff_attn_kernel, lam_init=lam_init, n_ctx=n_ctx, ctx_tiles=ctx_tiles)
    kv = pl.BlockSpec((None, hp, n_rows, HEAD_LANES), lambda b, h, t: (b, h, 0, 0))
    vec = _full(lam_vecs[0].shape)
    return pl.pallas_call(
        kern,
        grid=(n_batch, n_heads // hp, q_tiles),
        in_specs=[
            pl.BlockSpec((None, hp, tq, HEAD_LANES), lambda b, h, t: (b, h, t + q_off, 0)),
            kv, kv, vec, vec, vec, vec, _full((1, HEAD_LANES)),
        ],
        out_specs=pl.BlockSpec((None, tq, hp * HEAD_LANES), lambda b, h, t: (b, t, h)),
        out_shape=jax.ShapeDtypeStruct((n_batch, q_tiles * tq, n_heads * HEAD_LANES), BF16),
        compiler_params=_cparams("arbitrary", "arbitrary", "arbitrary"),
        name="diff_attention",
    )(q, k, v, *lam_vecs, sub_gain)


def _oproj_kernel(o_ref, w_ref, h_ref, gt_ref, out_ref):
    out_ref[...] = h_ref[...] + gt_ref[...] * _dot(o_ref[...], w_ref[...])


def _out_proj(o, w, h, modl, tiles, n_tiles, h_off):
    n_batch, _, d = h.shape
    tm = tiles.tm
    return pl.pallas_call(
        _oproj_kernel,
        grid=(n_batch, n_tiles),
        in_specs=[tiles.rows(o.shape[-1]), _resident(w.shape), tiles.rows(d, h_off), tiles.mod(2, d)],
        out_specs=tiles.rows(d),
        out_shape=jax.ShapeDtypeStruct((n_batch, n_tiles * tm, d), F32),
        compiler_params=_cparams("arbitrary", "arbitrary"),
        name="out_proj",
    )(o, w, h, modl)


def _router_kernel(h_ref, sh_ref, sc_ref, g_ref, wr_ref, br_ref, mx_ref, ei_ref, wf_ref, cnt_ref,
                   carry_ref, *, n_groups, n_experts):
    first = jnp.logical_and(pl.program_id(0) == 0, pl.program_id(1) == 0)

    @pl.when(first)
    def _():
        carry_ref[...] = jnp.zeros_like(carry_ref)

    x = h_ref[...]
    tm = x.shape[0]
    mx = _norm_mod(x, g_ref[...], sh_ref[...], sc_ref[...])
    mx_ref[...] = _pack_bf16_pairs(mx)
    logits = _dot_f32(mx, wr_ref[...]) + br_ref[...]
    lane = lax.broadcasted_iota(jnp.int32, logits.shape, 1).astype(F32)
    neg = jnp.float32(-jnp.inf)
    big = jnp.float32(LANES)

    def first_argmax(vals):
        m = jnp.max(vals, axis=-1, keepdims=True)
        return m, jnp.min(jnp.where(vals == m, lane, big), axis=-1, keepdims=True)

    glog = jnp.where(lane < n_groups, logits, neg)
    gmax, gidx = first_argmax(glog)
    grp_w = 1.0 / jnp.sum(jnp.exp(glog - gmax), axis=-1, keepdims=True)
    e_lo = n_groups + gidx * n_experts
    elog = jnp.where(jnp.logical_and(lane >= e_lo, lane < e_lo + n_experts), logits, neg)
    v1, i1 = first_argmax(elog)
    v2, i2 = first_argmax(jnp.where(lane == i1, neg, elog))
    t = jnp.exp(v2 - v1)
    w1 = grp_w / (1.0 + t)
    w2 = grp_w * t / (1.0 + t)

    onehot = jnp.logical_or(lane == i1, lane == i2)
    r_io = lax.broadcasted_iota(jnp.int32, (tm, tm), 0)
    c_io = lax.broadcasted_iota(jnp.int32, (tm, tm), 1)
    tri = jnp.where(r_io >= c_io, 1.0, 0.0).astype(BF16)
    incl = _dot(tri, jnp.where(onehot, 1.0, 0.0).astype(BF16))
    before = carry_ref[...] + incl - 1.0
    r1 = jnp.sum(jnp.where(lane == i1, before, 0.0), axis=-1, keepdims=True)
    r2 = jnp.sum(jnp.where(lane == i2, before, 0.0), axis=-1, keepdims=True)
    carry = carry_ref[...] + incl[tm - 1:tm, :]
    carry_ref[...] = carry
    cnt_ref[...] = carry.astype(jnp.int32)

    ei = jnp.where(lane == 0, i1 - n_groups,
                   jnp.where(lane == 1, i2 - n_groups,
                             jnp.where(lane == 2, r1, jnp.where(lane == 3, r2, 0.0))))
    ei_ref[...] = ei.astype(jnp.int32)
    wf_ref[...] = jnp.where(lane == 0, w1, jnp.where(lane == 1, w2, 0.0))


def _norm_router(h, modl, g2, w_r, b_r, tiles, n_tiles, off, n_groups, n_experts):
    n_batch, _, d = h.shape
    tm = tiles.tm
    n_tok = n_batch * n_tiles * tm
    kern = functools.partial(_router_kernel, n_groups=n_groups, n_experts=n_experts)

    def flat(width):
        return pl.BlockSpec((tm, width), lambda b, t: (b * n_tiles + t, 0))

    return pl.pallas_call(
        kern,
        grid=(n_batch, n_tiles),
        in_specs=[tiles.rows(d, off), tiles.mod(3, d), tiles.mod(4, d), _full((1, d)),
                  _full(w_r.shape), _full((1, LANES))],
        out_specs=[flat(d // 2), flat(LANES), flat(LANES), _full((1, LANES))],
        out_shape=[
            jax.ShapeDtypeStruct((n_tok, d // 2), jnp.uint32),
            jax.ShapeDtypeStruct((n_tok, LANES), jnp.int32),
            jax.ShapeDtypeStruct((n_tok, LANES), F32),
            jax.ShapeDtypeStruct((1, LANES), jnp.int32),
        ],
        scratch_shapes=[pltpu.VMEM((1, LANES), F32)],
        compiler_params=_cparams("arbitrary", "arbitrary"),
        name="moe_norm_router",
    )(h, modl, modl, g2, w_r, b_r)


SUBLANES = 8


def _dispatch_kernel(pos_ref, fill_ref, mx_ref, xs_ref, zero_ref, sem, fill_sem, *, n_tok, top_k, tme):
    tm = mx_ref.shape[0] * SUBLANES
    i = pl.program_id(0)
    base = i * tm

    @pl.when(i == 0)
    def _():
        zero_ref[...] = jnp.zeros_like(zero_ref)

        def fill(wait):
            def body(j, carry):
                @pl.when(fill_ref[j] != 0)
                def _():
                    dst = xs_ref.at[pl.ds(pl.multiple_of(j * tme, tme), tme), :]
                    cp = pltpu.make_async_copy(zero_ref, dst, fill_sem)
                    cp.wait() if wait else cp.start()
                return carry
            lax.fori_loop(0, fill_ref.shape[0], body, 0)

        fill(False)
        fill(True)

    def rows(g, wait):
        for u in range(SUBLANES):
            for kk in range(top_k):
                dst = 0 if wait else pos_ref[kk * n_tok + base + g * SUBLANES + u]
                cp = pltpu.make_async_copy(mx_ref.at[g, pl.ds(u, 1), :],
                                           xs_ref.at[pl.ds(dst, 1), :], sem)
                cp.wait() if wait else cp.start(priority=kk % 2)

    def start(g, carry):
        rows(g, False)
        return carry

    def wait(g, carry):
        rows(0, True)
        return carry

    lax.fori_loop(0, tm // SUBLANES, start, 0)
    lax.fori_loop(0, tm // SUBLANES, wait, 0)


def _dispatch(pos, fill, mx, n_rows, tm, top_k, tme):
    n_tok, d = mx.shape
    kern = functools.partial(_dispatch_kernel, n_tok=n_tok, top_k=top_k, tme=tme)
    return pl.pallas_call(
        kern,
        grid_spec=pltpu.PrefetchScalarGridSpec(
            num_scalar_prefetch=2,
            grid=(n_tok // tm,),
            in_specs=[pl.BlockSpec((tm // SUBLANES, SUBLANES, d), lambda i, pos, fill: (i, 0, 0))],
            out_specs=pl.BlockSpec(memory_space=pl.ANY),
            scratch_shapes=[pltpu.VMEM((tme, d), mx.dtype), pltpu.SemaphoreType.DMA(()),
                            pltpu.SemaphoreType.DMA(())],
        ),
        out_shape=jax.ShapeDtypeStruct((n_rows, d), mx.dtype),
        compiler_params=_cparams("arbitrary"),
        name="moe_dispatch",
    )(pos, fill, mx.reshape(n_tok // SUBLANES, SUBLANES, d))


_ACTIVE, _FIRST, _SLOT = 1, 2, 4


def _ffn_kernel(te_ref, tn_ref, tx_ref, fl_ref, xs_ref, wg_hbm, wu_hbm, wd_hbm, ys_ref,
                wg_f, wu_f, wd_f, wg_b, wu_b, wd_b, sems):
    del tx_ref
    i = pl.program_id(0)
    flags = fl_ref[i]

    slot = jnp.where((flags & _SLOT) != 0, 1, 0)

    def weight_copies(e, s):
        return (pltpu.make_async_copy(wg_hbm.at[e], wg_f.at[s], sems.at[s, 0]),
                pltpu.make_async_copy(wu_hbm.at[e], wu_f.at[s], sems.at[s, 1]),
                pltpu.make_async_copy(wd_hbm.at[e], wd_f.at[s], sems.at[s, 2]))

    def start_weights(e, s):
        for cp, queue in zip(weight_copies(e, s), (1, 1, 0)):
            cp.start(priority=queue)

    @pl.when(jnp.logical_and(i == 0, (flags & _FIRST) != 0))
    def _():
        start_weights(te_ref[0], slot)

    @pl.when((flags & _FIRST) != 0)
    def _():
        for cp in weight_copies(te_ref[i], slot):
            cp.wait()

        @pl.when(tn_ref[i] >= 0)
        def _():
            start_weights(tn_ref[i], 1 - slot)

        wg_b[...] = wg_f[slot].astype(BF16)
        wu_b[...] = wu_f[slot].astype(BF16)
        wd_b[...] = wd_f[slot].astype(BF16)

    @pl.when((flags & _ACTIVE) != 0)
    def _():
        lo, hi = _unpack_bf16_pairs(xs_ref[...])
        x = jnp.concatenate([lo.astype(BF16), hi.astype(BF16)], axis=1)
        a = _dot(x, wg_b[...])
        b = _dot(x, wu_b[...])
        ys_ref[...] = _pack_bf16_pairs(_dot((jax.nn.silu(a) * b).astype(BF16), wd_b[...]))

    @pl.when((flags & _ACTIVE) == 0)
    def _():
        ys_ref[...] = jnp.zeros_like(ys_ref)


def _expert_ffn(tile_e, tile_next, tile_x, flags, xs, w_gate, w_up, w_down, tme):
    n_rows, packed = xs.shape
    n_tiles = n_rows // tme
    d, f = w_gate.shape[-2:]
    any_spec = pl.BlockSpec(memory_space=pl.ANY)
    return pl.pallas_call(
        _ffn_kernel,
        grid_spec=pltpu.PrefetchScalarGridSpec(
            num_scalar_prefetch=4,
            grid=(n_tiles,),
            in_specs=[pl.BlockSpec((tme, packed), lambda i, te, tn, tx, fl: (tx[i], 0)),
                      any_spec, any_spec, any_spec],
            out_specs=pl.BlockSpec((tme, packed), lambda i, te, tn, tx, fl: (i, 0)),
            scratch_shapes=[pltpu.VMEM((2, d, f), F32), pltpu.VMEM((2, d, f), F32),
                            pltpu.VMEM((2, f, d), F32),
                            pltpu.VMEM((d, f), BF16), pltpu.VMEM((d, f), BF16),
                            pltpu.VMEM((f, d), BF16), pltpu.SemaphoreType.DMA((2, 3))],
        ),
        out_shape=jax.ShapeDtypeStruct((n_rows, packed), jnp.uint32),
        compiler_params=_cparams("arbitrary"),
        name="moe_expert_ffn",
    )(tile_e, tile_next, tile_x, flags, xs, w_gate, w_up, w_down)


def _combine_kernel(pos_ref, ys_ref, h_ref, gt_ref, wf_ref, o_ref, buf, sems, *, n_tok, top_k):
    tm = h_ref.shape[0]
    i = pl.program_id(0)
    n = pl.num_programs(0)

    def gather(tile, slot, wait):
        def body(g, carry):
            for u in range(SUBLANES):
                for kk in range(top_k):
                    src = 0 if wait else pos_ref[kk * n_tok + tile * tm + g * SUBLANES + u]
                    cp = pltpu.make_async_copy(ys_ref.at[pl.ds(src, 1), :],
                                               buf.at[slot, kk, 0 if wait else g, pl.ds(u, 1), :],
                                               sems.at[slot])
                    cp.wait() if wait else cp.start(priority=kk % 2)
            return carry
        lax.fori_loop(0, tm // SUBLANES, body, 0)

    @pl.when(i == 0)
    def _():
        gather(0, 0, False)

    @pl.when(i + 1 < n)
    def _():
        gather(i + 1, (i + 1) % 2, False)

    slot = i % 2
    gather(i, slot, True)
    wf = wf_ref[...]
    half = h_ref.shape[1] // 2
    y_lo = y_hi = None
    for kk in range(top_k):
        lo, hi = _unpack_bf16_pairs(buf[slot, kk].reshape(tm, half))
        w = wf[:, kk:kk + 1]
        y_lo = lo * w if y_lo is None else y_lo + lo * w
        y_hi = hi * w if y_hi is None else y_hi + hi * w
    o_ref[:, :half] = h_ref[:, :half] + gt_ref[:, :half] * y_lo
    o_ref[:, half:] = h_ref[:, half:] + gt_ref[:, half:] * y_hi


def _combine(pos, ys, h, modl, wf, tiles, n_tiles, off, top_k):
    n_batch, _, d = h.shape
    tm = tiles.tm
    n_tok = n_batch * n_tiles * tm
    nb, ct = n_batch, tiles.ctx_tiles
    kern = functools.partial(_combine_kernel, n_tok=n_tok, top_k=top_k)

    def bt(i):
        return i // n_tiles, i % n_tiles

    return pl.pallas_call(
        kern,
        grid_spec=pltpu.PrefetchScalarGridSpec(
            num_scalar_prefetch=1,
            grid=(n_batch * n_tiles,),
            in_specs=[
                pl.BlockSpec(memory_space=pl.ANY),
                pl.BlockSpec((None, tm, d), lambda i, pos: (bt(i)[0], bt(i)[1] + off, 0)),
                pl.BlockSpec((None, None, 1, d),
                             lambda i, pos: (5, jnp.where(bt(i)[1] < ct, nb, bt(i)[0]), 0, 0)),
                pl.BlockSpec((tm, LANES), lambda i, pos: (i, 0)),
            ],
            out_specs=pl.BlockSpec((None, tm, d), lambda i, pos: (bt(i)[0], bt(i)[1], 0)),
            scratch_shapes=[pltpu.VMEM((2, top_k, tm // SUBLANES, SUBLANES, d // 2), jnp.uint32),
                            pltpu.SemaphoreType.DMA((2,))],
        ),
        out_shape=jax.ShapeDtypeStruct((n_batch, n_tiles * tm, d), F32),
        compiler_params=_cparams("arbitrary"),
        name="moe_combine",
    )(pos, ys, h, modl, wf)


def _moe_layer(layer, h, modl, g2, w_grp, b_grp, w_exp, b_exp, w_gate, w_up, w_down, tiles, n_tiles, off):
    n_batch, _, d = h.shape
    tm = tiles.tm
    n_groups, n_experts = w_gate.shape[1], w_gate.shape[2]
    n_all = n_groups * n_experts
    top_k = 2
    tme = MOE_ROW_TILE
    n_tok = n_batch * n_tiles * tm

    w_r = jnp.zeros((d, LANES), F32).at[:, :n_groups].set(w_grp).at[:, n_groups:n_groups + n_all].set(w_exp)
    b_r = jnp.zeros((1, LANES), F32).at[0, :n_groups].set(b_grp).at[0, n_groups:n_groups + n_all].set(b_exp)
    mx, ei, wf, cnt = _norm_router(h, modl, g2, w_r, b_r, tiles, n_tiles, off, n_groups, n_experts)

    counts = cnt[0, n_groups:n_groups + n_all]
    padded = ((counts + tme - 1) // tme) * tme
    ends = jnp.cumsum(padded)
    offs = ends - padded
    eid = jnp.arange(n_all, dtype=jnp.int32)
    row0 = jnp.sum(jnp.where(ei[:, :top_k, None] == eid, offs.astype(jnp.int32), 0), axis=-1)
    pos = (row0 + ei[:, top_k:2 * top_k]).T.reshape(-1).astype(jnp.int32)
    n_tile_max = (top_k * n_tok + n_all * (tme - 1)) // tme + 1
    n_rows = n_tile_max * tme
    n_active = ends[-1] // tme
    last = jnp.maximum(n_active - 1, 0)
    tidx = jnp.arange(n_tile_max, dtype=jnp.int32)
    active = tidx < n_active
    tile_x = jnp.where(active, tidx, last).astype(jnp.int32)
    tile_e = jnp.sum((ends[None, :] <= (tile_x * tme)[:, None]).astype(jnp.int32), axis=1)
    tile_e = jnp.minimum(tile_e, n_all - 1)
    prev_e = jnp.concatenate([jnp.full((1,), -1, jnp.int32), tile_e[:-1]])
    first = jnp.logical_and(active, tile_e != prev_e)
    parity = (jnp.cumsum(first.astype(jnp.int32)) + 1) % 2
    flags = (active.astype(jnp.int32) * _ACTIVE + first.astype(jnp.int32) * _FIRST
             + parity * _SLOT).astype(jnp.int32)
    later = jnp.logical_and(eid[None, :] > eid[:, None], (padded > 0)[None, :])
    next_e = jnp.min(jnp.where(later, eid[None, :], n_all), axis=1)
    next_e = jnp.where(next_e < n_all, next_e + layer * n_all, -1).astype(jnp.int32)
    tile_next = jnp.sum(jnp.where(tile_e[:, None] == eid[None, :], next_e[None, :], 0), axis=1)
    tile_w = (tile_e + layer * n_all).astype(jnp.int32)

    next_tile_e = jnp.concatenate([tile_e[1:], jnp.full((1,), -1, jnp.int32)])
    last_of_expert = jnp.logical_or(tile_e != next_tile_e, tidx + 1 >= n_active)
    fill = jnp.logical_or(jnp.logical_not(active), last_of_expert).astype(jnp.int32)

    xs = _dispatch(pos, fill, mx, n_rows, tm, top_k, tme)
    ys = _expert_ffn(tile_w, tile_next.astype(jnp.int32), tile_x, flags, xs,
                     w_gate.reshape(-1, d, w_gate.shape[-1]), w_up.reshape(-1, d, w_up.shape[-1]),
                     w_down.reshape(-1, w_down.shape[-2], d), tme)
    return _combine(pos, ys, h, modl, wf, tiles, n_tiles, off, top_k)


def _rope_tables(n_ctx, n_lat, head_dim):
    rows = n_lat // GRID_W
    row_id = jnp.repeat(jnp.arange(rows, dtype=F32), GRID_W)
    col_id = jnp.tile(jnp.arange(GRID_W, dtype=F32), rows)
    half = head_dim // 2
    inv_freq = ROPE_THETA ** (-jnp.arange(0, half, 2, dtype=F32) / half)
    ang_r = row_id[:, None] * inv_freq[None, :]
    ang_c = col_id[:, None] * inv_freq[None, :]
    ang = jnp.concatenate([ang_r, ang_r, ang_c, ang_c], axis=-1)
    reps = HEAD_LANES // head_dim
    cos = jnp.tile(jnp.cos(ang), (1, reps))
    sin = jnp.tile(jnp.sin(ang), (1, reps))
    quarter = head_dim // 4
    first = (jnp.arange(HEAD_LANES) % (2 * quarter)) < quarter
    s_dn = jnp.where(first[None, :], -sin, 0.0)
    s_up = jnp.where(first[None, :], 0.0, sin)
    pad = lambda t, fill: jnp.concatenate([jnp.full((n_ctx, HEAD_LANES), fill, F32), t], axis=0)
    return pad(cos, 1.0), pad(s_dn, 0.0), pad(s_up, 0.0)


def kernel(x, c, ctx, c_ctx, ada_w, ada_b, norm1_g, norm2_g, gmlp_w_in, gmlp_v_gain, gmlp_w_s, gmlp_b_s, gmlp_w_out, gqa_w_qkv, gqa_q_gain, gqa_k_gain, gqa_w_o, diff_w_qkv, diff_q_gain, diff_k_gain, diff_lam_q1, diff_lam_k1, diff_lam_q2, diff_lam_k2, diff_sub_gain, diff_w_o, moe_w_grp, moe_b_grp, moe_w_exp, moe_b_exp, moe_w_gate, moe_w_up, moe_w_down):
    n_batch, n_lat, d = x.shape
    n_ctx = ctx.shape[1]
    depth = ada_w.shape[0]
    tm = 256 if (n_ctx % 256 == 0 and n_lat % 256 == 0) else 128
    assert n_ctx % tm == 0 and n_lat % tm == 0 and tm % CHUNK == 0 and d % LANES == 0
    ctx_tiles, lat_tiles = n_ctx // tm, n_lat // tm

    mod_rows = -(-(n_batch + 1) // 8) * 8
    cc = jnp.zeros((mod_rows, d), F32).at[:n_batch].set(c).at[n_batch].set(c_ctx)
    mod = _ada_table(cc, ada_w, ada_b)
    mod = mod.reshape(depth, mod_rows, 6, 1, d).transpose(0, 2, 1, 3, 4)

    gqa_heads = gqa_w_o.shape[1] // HEAD_LANES
    gqa_kv_heads = (gqa_w_qkv.shape[2] // HEAD_LANES - gqa_heads) // 2
    diff_heads = diff_w_o.shape[1] // HEAD_LANES
    rope_b = _rope_tables(n_ctx, n_lat, HEAD_LANES)
    rope_c = _rope_tables(n_ctx, n_lat, HEAD_LANES // 2)

    h = None
    has_ctx = True
    for i in range(depth):
        kind, j = i % N_MIXERS, i // N_MIXERS
        need_ctx = any((k % N_MIXERS) != 0 for k in range(i + 1, depth))
        modl = mod[i]
        g1 = norm1_g[i].reshape(1, d)
        g2 = norm2_g[i].reshape(1, d)
        in_tiles = _Tiles(n_batch, tm, ctx_tiles if has_ctx else 0)
        n_in = (ctx_tiles if has_ctx else 0) + lat_tiles
        keep = has_ctx and need_ctx
        out_tiles = _Tiles(n_batch, tm, ctx_tiles if keep else 0)
        n_out = (ctx_tiles if keep else 0) + lat_tiles
        off = n_in - n_out

        if kind == 0:
            srcs = (ctx, x) if h is None else (h,)
            h = _gmlp_layer(srcs, modl, g1, gmlp_w_in[j].astype(BF16), gmlp_v_gain[j].reshape(1, -1),
                            gmlp_w_s[j].astype(BF16), gmlp_b_s[j].T, gmlp_w_out[j].astype(BF16),
                            out_tiles, n_out, off)
        else:
            if h is None:
                h = jnp.concatenate([ctx, x], axis=1)
            if kind == 1:
                q, k, v = _qkv_proj(
                    h, modl, g1, gqa_w_qkv[j].astype(BF16), gqa_q_gain[j].reshape(1, -1),
                    gqa_k_gain[j].reshape(1, -1), rope_b if has_ctx else tuple(t[n_ctx:] for t in rope_b),
                    in_tiles, n_in, n_q=gqa_heads, n_k=gqa_kv_heads, n_v=gqa_kv_heads, halves=1,
                    quarter=HEAD_LANES // 4, q_scale=LOG2E / math.sqrt(HEAD_LANES))
                o = _gqa_attention(q, k, v, tm, n_ctx if has_ctx else 0, n_out, off)
                w_o = gqa_w_o[j]
            else:
                lam_init = 0.8 - 0.6 * math.exp(-0.3 * i)
                q, k, v = _qkv_proj(
                    h, modl, g1, diff_w_qkv[j].astype(BF16), diff_q_gain[j].reshape(1, -1),
                    diff_k_gain[j].reshape(1, -1), rope_c if has_ctx else tuple(t[n_ctx:] for t in rope_c),
                    in_tiles, n_in, n_q=diff_heads, n_k=diff_heads, n_v=diff_heads, halves=2,
                    quarter=HEAD_LANES // 8, q_scale=LOG2E / math.sqrt(HEAD_LANES // 2))
                lam_vecs = [t[j].reshape(1, -1) for t in (diff_lam_q1, diff_lam_k1, diff_lam_q2, diff_lam_k2)]
                o = _diff_attention(q, k, v, lam_vecs, diff_sub_gain[j].reshape(1, -1), lam_init, tm,
                                    n_ctx if has_ctx else 0, n_out, off)
                w_o = diff_w_o[j]
            h = _out_proj(o, w_o.astype(BF16), h, modl, out_tiles, n_out, off)

        h = _moe_layer(i, h, modl, g2, moe_w_grp[i], moe_b_grp[i], moe_w_exp[i], moe_b_exp[i],
                       moe_w_gate, moe_w_up, moe_w_down, out_tiles, n_out, 0)
        has_ctx = keep
    return h[:, n_ctx:] if has_ctx else h
```

```python
import functools
import math

import jax
import jax.numpy as jnp
from jax import lax
from jax.experimental import pallas as pl
from jax.experimental.pallas import tpu as pltpu

EPS = 1e-6
GRID_W = 64
ROPE_THETA = 10000.0
N_MIXERS = 3
CHUNK = 128
LANES = 128
HEAD_LANES = 128
MXU_COLS = 256
MOE_ROW_TILE = 128
DIFF_HEADS_PER_STEP = 8
GQA_KV_HEADS_PER_STEP = 8
VMEM_LIMIT_BYTES = 60000 * 1024
ADA_COL_TILE = 1024
LOG2E = math.log2(math.e)

F32 = jnp.float32
BF16 = jnp.bfloat16


def _cparams(*sem):
    return pltpu.CompilerParams(dimension_semantics=sem, vmem_limit_bytes=VMEM_LIMIT_BYTES)


def _resident(shape):
    nd = len(shape)
    return pl.BlockSpec(shape, lambda *_: (0,) * nd, pipeline_mode=pl.Buffered(1))


def _full(shape):
    nd = len(shape)
    return pl.BlockSpec(shape, lambda *_: (0,) * nd)


def _norm_mod(x, g, shift, scale):
    y = x * lax.rsqrt(jnp.mean(x * x, axis=-1, keepdims=True) + EPS)
    return (y * g) * (1.0 + scale) + shift


def _dot(a, b):
    return jnp.dot(a, b, preferred_element_type=F32)


def _split_bf16(x):
    hi = x.astype(BF16)
    return hi, (x - hi.astype(F32)).astype(BF16)


def _dot_f32(a, b):
    a_hi, a_lo = _split_bf16(a)
    b_hi, b_lo = _split_bf16(b)
    rows = a.shape[0]
    top = _dot(jnp.concatenate([a_hi, a_lo], axis=0), b_hi)
    return top[:rows] + top[rows:] + _dot(a_hi, b_lo)


def _ada_kernel(cc_ref, w_ref, b_ref, o_ref):
    s = jax.nn.silu(cc_ref[...])
    o_ref[...] = _dot_f32(s, w_ref[...]) + b_ref[...]


def _ada_table(cc, ada_w, ada_b):
    depth, d, n6 = ada_w.shape
    rows = cc.shape[0]
    tn = ADA_COL_TILE
    return pl.pallas_call(
        _ada_kernel,
        grid=(depth, n6 // tn),
        in_specs=[
            pl.BlockSpec((rows, d), lambda l, n: (0, 0)),
            pl.BlockSpec((None, d, tn), lambda l, n: (l, 0, n)),
            pl.BlockSpec((None, 1, tn), lambda l, n: (l, 0, n)),
        ],
        out_specs=pl.BlockSpec((None, rows, tn), lambda l, n: (l, 0, n)),
        out_shape=jax.ShapeDtypeStruct((depth, rows, n6), F32),
        compiler_params=_cparams("arbitrary", "arbitrary"),
        name="ada_table",
    )(cc, ada_w, ada_b.reshape(depth, 1, n6))


class _Tiles:
    def __init__(self, n_batch, tm, ctx_tiles):
        self.n_batch, self.tm, self.ctx_tiles = n_batch, tm, ctx_tiles

    def mod(self, j, d):
        nb, ct = self.n_batch, self.ctx_tiles
        return pl.BlockSpec((None, None, 1, d), lambda b, t: (j, jnp.where(t < ct, nb, b), 0, 0))

    def rows(self, d, off=0):
        tm = self.tm
        return pl.BlockSpec((None, tm, d), lambda b, t: (b, t + off, 0))


def _gmlp_kernel(*refs, width, groups, ctx_tiles, split):
    if split:
        c_ref, l_ref = refs[:2]
        x = jnp.where(pl.program_id(1) < ctx_tiles, c_ref[...], l_ref[...])
        refs = refs[2:]
    else:
        x = refs[0][...]
        refs = refs[1:]
    sh_ref, sc_ref, gt_ref, g_ref, win_ref, vg_ref, ws_ref, bs_ref, wout_ref, o_ref, gated_ref = refs
    tm = x.shape[0]
    a = _norm_mod(x, g_ref[...], sh_ref[...], sc_ref[...]).astype(BF16)
    v = jax.nn.gelu(_dot(a, win_ref[:, width:]))
    v = v * lax.rsqrt(jnp.mean(v * v, axis=-1, keepdims=True) + EPS) * vg_ref[...]
    v = v.astype(BF16)
    gd = width // groups
    per = MXU_COLS // gd
    for g0 in range(0, groups, per):
        u = jax.nn.gelu(_dot(a, win_ref[:, g0 * gd:(g0 + per) * gd]))
        for g in range(g0, g0 + per):
            cols = slice(g * gd, (g + 1) * gd)
            ug = u[:, (g - g0) * gd:(g - g0 + 1) * gd]
            for c in range(tm // CHUNK):
                rows = slice(c * CHUNK, (c + 1) * CHUNK)
                s = _dot(ws_ref[g], v[rows, cols]) + bs_ref[:, g:g + 1]
                gated_ref[rows, cols] = (ug[rows] * s).astype(BF16)
    y = _dot(gated_ref[...], wout_ref[...])
    o_ref[...] = x + gt_ref[...] * y


def _gmlp_layer(srcs, modl, g1, w_in, v_gain, w_s, b_s, w_out, tiles, n_tiles, off):
    n_batch, _, d = srcs[-1].shape
    tm = tiles.tm
    width = w_out.shape[0]
    groups = w_s.shape[0]
    split = len(srcs) == 2
    ct = tiles.ctx_tiles
    kern = functools.partial(_gmlp_kernel, width=width, groups=groups, ctx_tiles=ct, split=split)
    if split:
        src_specs = [pl.BlockSpec((None, tm, d), lambda b, t: (b, jnp.minimum(t, ct - 1), 0)),
                     pl.BlockSpec((None, tm, d), lambda b, t: (b, jnp.maximum(t - ct, 0), 0))]
    else:
        src_specs = [tiles.rows(d, off)]
    return pl.pallas_call(
        kern,
        grid=(n_batch, n_tiles),
        in_specs=src_specs + [
            tiles.mod(0, d), tiles.mod(1, d), tiles.mod(2, d),
            _full((1, d)), _resident(w_in.shape), _full((1, width)), _resident(w_s.shape),
            _full(b_s.shape), _resident(w_out.shape),
        ],
        out_specs=tiles.rows(d),
        out_shape=jax.ShapeDtypeStruct((n_batch, n_tiles * tm, d), F32),
        scratch_shapes=[pltpu.VMEM((tm, width), BF16)],
        compiler_params=_cparams("arbitrary", "arbitrary"),
        name="gmlp_mixer",
    )(*srcs, modl, modl, modl, g1, w_in, v_gain, w_s, b_s, w_out)


def _rope(x, cos, s_dn, s_up, quarter):
    return (x * cos + pltpu.roll(x, LANES - quarter, 1) * s_dn + pltpu.roll(x, quarter, 1) * s_up)


def _qkv_kernel(h_ref, sh_ref, sc_ref, g_ref, w_ref, qg_ref, kg_ref, cos_ref, sdn_ref, sup_ref,
                q_ref, k_ref, v_ref, *, n_q, n_k, n_v, halves, quarter, q_scale):
    x = h_ref[...]
    a = _norm_mod(x, g_ref[...], sh_ref[...], sc_ref[...]).astype(BF16)
    cos, sdn, sup = cos_ref[...], sdn_ref[...], sup_ref[...]
    lane = lax.broadcasted_iota(jnp.int32, (1, HEAD_LANES), 1)
    lo = lane < (HEAD_LANES // 2)

    def head_norm(t, gain):
        sq = t * t
        if halves == 1:
            inv = lax.rsqrt(jnp.mean(sq, axis=-1, keepdims=True) + EPS)
        else:
            tot = jnp.sum(sq, axis=-1, keepdims=True)
            s_lo = jnp.sum(jnp.where(lo, sq, 0.0), axis=-1, keepdims=True)
            half = HEAD_LANES // 2
            inv = jnp.where(lo, lax.rsqrt(s_lo / half + EPS), lax.rsqrt((tot - s_lo) / half + EPS))
        return t * inv * gain

    qg, kg = qg_ref[...], kg_ref[...]
    per = MXU_COLS // HEAD_LANES
    for j0 in range(0, n_q + n_k + n_v, per):
        t2 = _dot(a, w_ref[:, j0 * HEAD_LANES:(j0 + per) * HEAD_LANES])
        for j in range(j0, j0 + per):
            t = t2[:, (j - j0) * HEAD_LANES:(j - j0 + 1) * HEAD_LANES]
            if j < n_q:
                t = _rope(head_norm(t, qg), cos, sdn, sup, quarter) * q_scale
                q_ref[j] = t.astype(BF16)
            elif j < n_q + n_k:
                k_ref[j - n_q] = _rope(head_norm(t, kg), cos, sdn, sup, quarter).astype(BF16)
            else:
                v_ref[j - n_q - n_k] = t.astype(BF16)


def _qkv_proj(h, modl, g1, w, q_gain, k_gain, rope_tabs, tiles, n_tiles, *, n_q, n_k, n_v, halves,
              quarter, q_scale):
    n_batch, n_rows, d = h.shape
    tm = tiles.tm
    cos, sdn, sup = rope_tabs
    kern = functools.partial(_qkv_kernel, n_q=n_q, n_k=n_k, n_v=n_v, halves=halves,
                             quarter=quarter, q_scale=q_scale)
    tab = pl.BlockSpec((tm, HEAD_LANES), lambda b, t: (t, 0))

    def heads(n):
        return pl.BlockSpec((None, n, tm, HEAD_LANES), lambda b, t: (b, 0, t, 0))

    def out(n):
        return jax.ShapeDtypeStruct((n_batch, n, n_rows, HEAD_LANES), BF16)

    return pl.pallas_call(
        kern,
        grid=(n_batch, n_tiles),
        in_specs=[
            tiles.rows(d), tiles.mod(0, d), tiles.mod(1, d), _full((1, d)), _resident(w.shape),
            _full((1, HEAD_LANES)), _full((1, HEAD_LANES)), tab, tab, tab,
        ],
        out_specs=[heads(n_q), heads(n_k), heads(n_v)],
        out_shape=[out(n_q), out(n_k), out(n_v)],
        compiler_params=_cparams("arbitrary", "arbitrary"),
        name="qkv_proj",
    )(h, modl, modl, g1, w, q_gain, k_gain, cos, sdn, sup)


def _softmax_pv(q, k, v):
    s = lax.dot_general(q, k, (((1,), (1,)), ((), ())), preferred_element_type=F32)
    p = jnp.exp2(s - jnp.max(s, axis=-1, keepdims=True))
    o = _dot(p.astype(BF16), v)
    return o / jnp.sum(p, axis=-1, keepdims=True)


def _gqa_attn_kernel(q_ref, k_ref, v_ref, o_ref, *, group, n_ctx, ctx_tiles):
    qi = pl.program_id(2)

    def attend(n_keys):
        for j in range(k_ref.shape[0]):
            k, v = k_ref[j, :n_keys], v_ref[j, :n_keys]
            for g in range(group):
                head = j * group + g
                o = _softmax_pv(q_ref[head], k, v)
                o_ref[:, head * HEAD_LANES:(head + 1) * HEAD_LANES] = o.astype(BF16)

    if ctx_tiles:
        pl.when(qi < ctx_tiles)(lambda: attend(n_ctx))
        pl.when(qi >= ctx_tiles)(lambda: attend(k_ref.shape[1]))
    else:
        attend(k_ref.shape[1])


def _gqa_attention(q, k, v, tq, n_ctx, q_tiles, q_off):
    n_batch, n_heads, n_rows, _ = q.shape
    kv_heads = k.shape[1]
    group = n_heads // kv_heads
    hp = min(GQA_KV_HEADS_PER_STEP, kv_heads)
    assert kv_heads % hp == 0
    ctx_tiles = 0 if q_off else n_ctx // tq
    kern = functools.partial(_gqa_attn_kernel, group=group, n_ctx=n_ctx, ctx_tiles=ctx_tiles)
    kv = pl.BlockSpec((None, hp, n_rows, HEAD_LANES), lambda b, h, t: (b, h, 0, 0))
    return pl.pallas_call(
        kern,
        grid=(n_batch, kv_heads // hp, q_tiles),
        in_specs=[
            pl.BlockSpec((None, hp * group, tq, HEAD_LANES), lambda b, h, t: (b, h, t + q_off, 0)),
            kv, kv,
        ],
        out_specs=pl.BlockSpec((None, tq, hp * group * HEAD_LANES), lambda b, h, t: (b, t, h)),
        out_shape=jax.ShapeDtypeStruct((n_batch, q_tiles * tq, n_heads * HEAD_LANES), BF16),
        compiler_params=_cparams("arbitrary", "arbitrary", "arbitrary"),
        name="gqa_attention",
    )(q, k, v)


def _diff_attn_kernel(q_ref, k_ref, v_ref, lq1_ref, lk1_ref, lq2_ref, lk2_ref, sg_ref, o_ref, *,
                      lam_init, n_ctx, ctx_tiles):
    qi = pl.program_id(2)
    lam = (jnp.exp(jnp.sum(lq1_ref[...] * lk1_ref[...], axis=-1, keepdims=True))
           - jnp.exp(jnp.sum(lq2_ref[...] * lk2_ref[...], axis=-1, keepdims=True)) + lam_init)
    lane = lax.broadcasted_iota(jnp.int32, (1, HEAD_LANES), 1)
    lo = lane < (HEAD_LANES // 2)

    def attend(n_keys):
        dims = (((1,), (1,)), ((), ()))
        for j in range(q_ref.shape[0]):
            q, k, v = q_ref[j], k_ref[j, :n_keys], v_ref[j, :n_keys]
            zero = jnp.zeros_like(q)
            s0 = lax.dot_general(jnp.where(lo, q, zero), k, dims, preferred_element_type=F32)
            s1 = lax.dot_general(jnp.where(lo, zero, q), k, dims, preferred_element_type=F32)
            e0 = jnp.exp2(s0 - jnp.max(s0, axis=-1, keepdims=True))
            e1 = jnp.exp2(s1 - jnp.max(s1, axis=-1, keepdims=True))
            c0 = 1.0 / jnp.sum(e0, axis=-1, keepdims=True)
            c1 = lam / jnp.sum(e1, axis=-1, keepdims=True)
            o = _dot(e0.astype(BF16), v) * c0 - _dot(e1.astype(BF16), v) * c1
            o = o * lax.rsqrt(jnp.mean(o * o, axis=-1, keepdims=True) + EPS) * sg_ref[...]
            o_ref[:, j * HEAD_LANES:(j + 1) * HEAD_LANES] = (o * (1.0 - lam_init)).astype(BF16)

    if ctx_tiles:
        pl.when(qi < ctx_tiles)(lambda: attend(n_ctx))
        pl.when(qi >= ctx_tiles)(lambda: attend(k_ref.shape[1]))
    else:
        attend(k_ref.shape[1])


def _diff_attention(q, k, v, lam_vecs, sub_gain, lam_init, tq, n_ctx, q_tiles, q_off):
    n_batch, n_heads, n_rows, _ = q.shape
    ctx_tiles = 0 if q_off else n_ctx // tq
    hp = min(DIFF_HEADS_PER_STEP, n_heads)
    assert n_heads % hp == 0
    kern = functools.partial(_diff_attn_kernel, lam_init=lam_init, n_ctx=n_ctx, ctx_tiles=ctx_tiles)
    kv = pl.BlockSpec((None, hp, n_rows, HEAD_LANES), lambda b, h, t: (b, h, 0, 0))
    vec = _full(lam_vecs[0].shape)
    return pl.pallas_call(
        kern,
        grid=(n_batch, n_heads // hp, q_tiles),
        in_specs=[
            pl.BlockSpec((None, hp, tq, HEAD_LANES), lambda b, h, t: (b, h, t + q_off, 0)),
            kv, kv, vec, vec, vec, vec, _full((1, HEAD_LANES)),
        ],
        out_specs=pl.BlockSpec((None, tq, hp * HEAD_LANES), lambda b, h, t: (b, t, h)),
        out_shape=jax.ShapeDtypeStruct((n_batch, q_tiles * tq, n_heads * HEAD_LANES), BF16),
        compiler_params=_cparams("arbitrary", "arbitrary", "arbitrary"),
        name="diff_attention",
    )(q, k, v, *lam_vecs, sub_gain)


def _oproj_kernel(o_ref, w_ref, h_ref, gt_ref, out_ref):
    out_ref[...] = h_ref[...] + gt_ref[...] * _dot(o_ref[...], w_ref[...])


def _out_proj(o, w, h, modl, tiles, n_tiles, h_off):
    n_batch, _, d = h.shape
    tm = tiles.tm
    return pl.pallas_call(
        _oproj_kernel,
        grid=(n_batch, n_tiles),
        in_specs=[tiles.rows(o.shape[-1]), _resident(w.shape), tiles.rows(d, h_off), tiles.mod(2, d)],
        out_specs=tiles.rows(d),
        out_shape=jax.ShapeDtypeStruct((n_batch, n_tiles * tm, d), F32),
        compiler_params=_cparams("arbitrary", "arbitrary"),
        name="out_proj",
    )(o, w, h, modl)


def _router_kernel(h_ref, sh_ref, sc_ref, g_ref, wr_ref, br_ref, mx_ref, ei_ref, wf_ref, cnt_ref,
                   carry_ref, *, n_groups, n_experts):
    first = jnp.logical_and(pl.program_id(0) == 0, pl.program_id(1) == 0)

    @pl.when(first)
    def _():
        carry_ref[...] = jnp.zeros_like(carry_ref)

    x = h_ref[...]
    tm = x.shape[0]
    mx = _norm_mod(x, g_ref[...], sh_ref[...], sc_ref[...])
    mx_ref[...] = mx
    logits = _dot_f32(mx, wr_ref[...]) + br_ref[...]
    lane = lax.broadcasted_iota(jnp.int32, logits.shape, 1).astype(F32)
    neg = jnp.float32(-jnp.inf)
    big = jnp.float32(LANES)

    def first_argmax(vals):
        m = jnp.max(vals, axis=-1, keepdims=True)
        return m, jnp.min(jnp.where(vals == m, lane, big), axis=-1, keepdims=True)

    glog = jnp.where(lane < n_groups, logits, neg)
    gmax, gidx = first_argmax(glog)
    grp_w = 1.0 / jnp.sum(jnp.exp(glog - gmax), axis=-1, keepdims=True)
    e_lo = n_groups + gidx * n_experts
    elog = jnp.where(jnp.logical_and(lane >= e_lo, lane < e_lo + n_experts), logits, neg)
    v1, i1 = first_argmax(elog)
    v2, i2 = first_argmax(jnp.where(lane == i1, neg, elog))
    t = jnp.exp(v2 - v1)
    w1 = grp_w / (1.0 + t)
    w2 = grp_w * t / (1.0 + t)

    onehot = jnp.logical_or(lane == i1, lane == i2)
    r_io = lax.broadcasted_iota(jnp.int32, (tm, tm), 0)
    c_io = lax.broadcasted_iota(jnp.int32, (tm, tm), 1)
    tri = jnp.where(r_io >= c_io, 1.0, 0.0).astype(BF16)
    incl = _dot(tri, jnp.where(onehot, 1.0, 0.0).astype(BF16))
    before = carry_ref[...] + incl - 1.0
    r1 = jnp.sum(jnp.where(lane == i1, before, 0.0), axis=-1, keepdims=True)
    r2 = jnp.sum(jnp.where(lane == i2, before, 0.0), axis=-1, keepdims=True)
    carry = carry_ref[...] + incl[tm - 1:tm, :]
    carry_ref[...] = carry
    cnt_ref[...] = carry.astype(jnp.int32)

    ei = jnp.where(lane == 0, i1 - n_groups,
                   jnp.where(lane == 1, i2 - n_groups,
                             jnp.where(lane == 2, r1, jnp.where(lane == 3, r2, 0.0))))
    ei_ref[...] = ei.astype(jnp.int32)
    wf_ref[...] = jnp.where(lane == 0, w1, jnp.where(lane == 1, w2, 0.0))


def _norm_router(h, modl, g2, w_r, b_r, tiles, n_tiles, off, n_groups, n_experts):
    n_batch, _, d = h.shape
    tm = tiles.tm
    n_tok = n_batch * n_tiles * tm
    kern = functools.partial(_router_kernel, n_groups=n_groups, n_experts=n_experts)

    def flat(width):
        return pl.BlockSpec((tm, width), lambda b, t: (b * n_tiles + t, 0))

    return pl.pallas_call(
        kern,
        grid=(n_batch, n_tiles),
        in_specs=[tiles.rows(d, off), tiles.mod(3, d), tiles.mod(4, d), _full((1, d)),
                  _full(w_r.shape), _full((1, LANES))],
        out_specs=[flat(d), flat(LANES), flat(LANES), _full((1, LANES))],
        out_shape=[
            jax.ShapeDtypeStruct((n_tok, d), F32),
            jax.ShapeDtypeStruct((n_tok, LANES), jnp.int32),
            jax.ShapeDtypeStruct((n_tok, LANES), F32),
            jax.ShapeDtypeStruct((1, LANES), jnp.int32),
        ],
        scratch_shapes=[pltpu.VMEM((1, LANES), F32)],
        compiler_params=_cparams("arbitrary", "arbitrary"),
        name="moe_norm_router",
    )(h, modl, modl, g2, w_r, b_r)


SUBLANES = 8


def _dispatch_kernel(pos_ref, fill_ref, mx_ref, xs_ref, zero_ref, sem, fill_sem, *, n_tok, top_k, tme):
    tm = mx_ref.shape[0] * SUBLANES
    i = pl.program_id(0)
    base = i * tm

    @pl.when(i == 0)
    def _():
        zero_ref[...] = jnp.zeros_like(zero_ref)

        def fill(wait):
            def body(j, carry):
                @pl.when(fill_ref[j] != 0)
                def _():
                    dst = xs_ref.at[pl.ds(pl.multiple_of(j * tme, tme), tme), :]
                    cp = pltpu.make_async_copy(zero_ref, dst, fill_sem)
                    cp.wait() if wait else cp.start()
                return carry
            lax.fori_loop(0, fill_ref.shape[0], body, 0)

        fill(False)
        fill(True)

    def rows(g, wait):
        for u in range(SUBLANES):
            for kk in range(top_k):
                dst = 0 if wait else pos_ref[kk * n_tok + base + g * SUBLANES + u]
                cp = pltpu.make_async_copy(mx_ref.at[g, pl.ds(u, 1), :],
                                           xs_ref.at[pl.ds(dst, 1), :], sem)
                cp.wait() if wait else cp.start(priority=kk % 2)

    def start(g, carry):
        rows(g, False)
        return carry

    def wait(g, carry):
        rows(0, True)
        return carry

    lax.fori_loop(0, tm // SUBLANES, start, 0)
    lax.fori_loop(0, tm // SUBLANES, wait, 0)


def _dispatch(pos, fill, mx, n_rows, tm, top_k, tme):
    n_tok, d = mx.shape
    kern = functools.partial(_dispatch_kernel, n_tok=n_tok, top_k=top_k, tme=tme)
    return pl.pallas_call(
        kern,
        grid_spec=pltpu.PrefetchScalarGridSpec(
            num_scalar_prefetch=2,
            grid=(n_tok // tm,),
            in_specs=[pl.BlockSpec((tm // SUBLANES, SUBLANES, d), lambda i, pos, fill: (i, 0, 0))],
            out_specs=pl.BlockSpec(memory_space=pl.ANY),
            scratch_shapes=[pltpu.VMEM((tme, d), mx.dtype), pltpu.SemaphoreType.DMA(()),
                            pltpu.SemaphoreType.DMA(())],
        ),
        out_shape=jax.ShapeDtypeStruct((n_rows, d), mx.dtype),
        compiler_params=_cparams("arbitrary"),
        name="moe_dispatch",
    )(pos, fill, mx.reshape(n_tok // SUBLANES, SUBLANES, d))


_ACTIVE, _FIRST, _SLOT = 1, 2, 4


def _ffn_kernel(te_ref, tn_ref, tx_ref, fl_ref, xs_ref, wg_hbm, wu_hbm, wd_hbm, ys_ref,
                wg_f, wu_f, wd_f, wg_b, wu_b, wd_b, sems):
    del tx_ref
    i = pl.program_id(0)
    flags = fl_ref[i]

    slot = jnp.where((flags & _SLOT) != 0, 1, 0)

    def weight_copies(e, s):
        return (pltpu.make_async_copy(wg_hbm.at[e], wg_f.at[s], sems.at[s, 0]),
                pltpu.make_async_copy(wu_hbm.at[e], wu_f.at[s], sems.at[s, 1]),
                pltpu.make_async_copy(wd_hbm.at[e], wd_f.at[s], sems.at[s, 2]))

    def start_weights(e, s):
        for cp, queue in zip(weight_copies(e, s), (1, 1, 0)):
            cp.start(priority=queue)

    @pl.when(jnp.logical_and(i == 0, (flags & _FIRST) != 0))
    def _():
        start_weights(te_ref[0], slot)

    @pl.when((flags & _FIRST) != 0)
    def _():
        for cp in weight_copies(te_ref[i], slot):
            cp.wait()

        @pl.when(tn_ref[i] >= 0)
        def _():
            start_weights(tn_ref[i], 1 - slot)

        wg_b[...] = wg_f[slot].astype(BF16)
        wu_b[...] = wu_f[slot].astype(BF16)
        wd_b[...] = wd_f[slot].astype(BF16)

    @pl.when((flags & _ACTIVE) != 0)
    def _():
        x = xs_ref[...].astype(BF16)
        a = _dot(x, wg_b[...])
        b = _dot(x, wu_b[...])
        ys_ref[...] = _dot((jax.nn.silu(a) * b).astype(BF16), wd_b[...])

    @pl.when((flags & _ACTIVE) == 0)
    def _():
        ys_ref[...] = jnp.zeros_like(ys_ref)


def _expert_ffn(tile_e, tile_next, tile_x, flags, xs, w_gate, w_up, w_down, tme):
    n_rows, d = xs.shape
    n_tiles = n_rows // tme
    f = w_gate.shape[-1]
    any_spec = pl.BlockSpec(memory_space=pl.ANY)
    return pl.pallas_call(
        _ffn_kernel,
        grid_spec=pltpu.PrefetchScalarGridSpec(
            num_scalar_prefetch=4,
            grid=(n_tiles,),
            in_specs=[pl.BlockSpec((tme, d), lambda i, te, tn, tx, fl: (tx[i], 0)),
                      any_spec, any_spec, any_spec],
            out_specs=pl.BlockSpec((tme, d), lambda i, te, tn, tx, fl: (i, 0)),
            scratch_shapes=[pltpu.VMEM((2, d, f), F32), pltpu.VMEM((2, d, f), F32),
                            pltpu.VMEM((2, f, d), F32),
                            pltpu.VMEM((d, f), BF16), pltpu.VMEM((d, f), BF16),
                            pltpu.VMEM((f, d), BF16), pltpu.SemaphoreType.DMA((2, 3))],
        ),
        out_shape=jax.ShapeDtypeStruct((n_rows, d), F32),
        compiler_params=_cparams("arbitrary"),
        name="moe_expert_ffn",
    )(tile_e, tile_next, tile_x, flags, xs, w_gate, w_up, w_down)


def _combine_kernel(pos_ref, ys_ref, h_ref, gt_ref, wf_ref, o_ref, buf, sems, *, n_tok, top_k):
    tm = h_ref.shape[0]
    i = pl.program_id(0)
    n = pl.num_programs(0)

    def gather(tile, slot, wait):
        def body(g, carry):
            for u in range(SUBLANES):
                for kk in range(top_k):
                    src = 0 if wait else pos_ref[kk * n_tok + tile * tm + g * SUBLANES + u]
                    cp = pltpu.make_async_copy(ys_ref.at[pl.ds(src, 1), :],
                                               buf.at[slot, kk, 0 if wait else g, pl.ds(u, 1), :],
                                               sems.at[slot])
                    cp.wait() if wait else cp.start(priority=kk % 2)
            return carry
        lax.fori_loop(0, tm // SUBLANES, body, 0)

    @pl.when(i == 0)
    def _():
        gather(0, 0, False)

    @pl.when(i + 1 < n)
    def _():
        gather(i + 1, (i + 1) % 2, False)

    slot = i % 2
    gather(i, slot, True)
    wf = wf_ref[...]
    d = h_ref.shape[1]
    y = buf[slot, 0].reshape(tm, d) * wf[:, 0:1]
    for kk in range(1, top_k):
        y = y + buf[slot, kk].reshape(tm, d) * wf[:, kk:kk + 1]
    o_ref[...] = h_ref[...] + gt_ref[...] * y


def _combine(pos, ys, h, modl, wf, tiles, n_tiles, off, top_k):
    n_batch, _, d = h.shape
    tm = tiles.tm
    n_tok = n_batch * n_tiles * tm
    nb, ct = n_batch, tiles.ctx_tiles
    kern = functools.partial(_combine_kernel, n_tok=n_tok, top_k=top_k)

    def bt(i):
        return i // n_tiles, i % n_tiles

    return pl.pallas_call(
        kern,
        grid_spec=pltpu.PrefetchScalarGridSpec(
            num_scalar_prefetch=1,
            grid=(n_batch * n_tiles,),
            in_specs=[
                pl.BlockSpec(memory_space=pl.ANY),
                pl.BlockSpec((None, tm, d), lambda i, pos: (bt(i)[0], bt(i)[1] + off, 0)),
                pl.BlockSpec((None, None, 1, d),
                             lambda i, pos: (5, jnp.where(bt(i)[1] < ct, nb, bt(i)[0]), 0, 0)),
                pl.BlockSpec((tm, LANES), lambda i, pos: (i, 0)),
            ],
            out_specs=pl.BlockSpec((None, tm, d), lambda i, pos: (bt(i)[0], bt(i)[1], 0)),
            scratch_shapes=[pltpu.VMEM((2, top_k, tm // SUBLANES, SUBLANES, d), F32),
                            pltpu.SemaphoreType.DMA((2,))],
        ),
        out_shape=jax.ShapeDtypeStruct((n_batch, n_tiles * tm, d), F32),
        compiler_params=_cparams("arbitrary"),
        name="moe_combine",
    )(pos, ys, h, modl, wf)


def _moe_layer(layer, h, modl, g2, w_grp, b_grp, w_exp, b_exp, w_gate, w_up, w_down, tiles, n_tiles, off):
    n_batch, _, d = h.shape
    tm = tiles.tm
    n_groups, n_experts = w_gate.shape[1], w_gate.shape[2]
    n_all = n_groups * n_experts
    top_k = 2
    tme = MOE_ROW_TILE
    n_tok = n_batch * n_tiles * tm

    w_r = jnp.zeros((d, LANES), F32).at[:, :n_groups].set(w_grp).at[:, n_groups:n_groups + n_all].set(w_exp)
    b_r = jnp.zeros((1, LANES), F32).at[0, :n_groups].set(b_grp).at[0, n_groups:n_groups + n_all].set(b_exp)
    mx, ei, wf, cnt = _norm_router(h, modl, g2, w_r, b_r, tiles, n_tiles, off, n_groups, n_experts)

    counts = cnt[0, n_groups:n_groups + n_all]
    padded = ((counts + tme - 1) // tme) * tme
    ends = jnp.cumsum(padded)
    offs = ends - padded
    eid = jnp.arange(n_all, dtype=jnp.int32)
    row0 = jnp.sum(jnp.where(ei[:, :top_k, None] == eid, offs.astype(jnp.int32), 0), axis=-1)
    pos = (row0 + ei[:, top_k:2 * top_k]).T.reshape(-1).astype(jnp.int32)
    n_tile_max = (top_k * n_tok + n_all * (tme - 1)) // tme + 1
    n_rows = n_tile_max * tme
    n_active = ends[-1] // tme
    last = jnp.maximum(n_active - 1, 0)
    tidx = jnp.arange(n_tile_max, dtype=jnp.int32)
    active = tidx < n_active
    tile_x = jnp.where(active, tidx, last).astype(jnp.int32)
    tile_e = jnp.sum((ends[None, :] <= (tile_x * tme)[:, None]).astype(jnp.int32), axis=1)
    tile_e = jnp.minimum(tile_e, n_all - 1)
    prev_e = jnp.concatenate([jnp.full((1,), -1, jnp.int32), tile_e[:-1]])
    first = jnp.logical_and(active, tile_e != prev_e)
    parity = (jnp.cumsum(first.astype(jnp.int32)) + 1) % 2
    flags = (active.astype(jnp.int32) * _ACTIVE + first.astype(jnp.int32) * _FIRST
             + parity * _SLOT).astype(jnp.int32)
    later = jnp.logical_and(eid[None, :] > eid[:, None], (padded > 0)[None, :])
    next_e = jnp.min(jnp.where(later, eid[None, :], n_all), axis=1)
    next_e = jnp.where(next_e < n_all, next_e + layer * n_all, -1).astype(jnp.int32)
    tile_next = jnp.sum(jnp.where(tile_e[:, None] == eid[None, :], next_e[None, :], 0), axis=1)
    tile_w = (tile_e + layer * n_all).astype(jnp.int32)

    next_tile_e = jnp.concatenate([tile_e[1:], jnp.full((1,), -1, jnp.int32)])
    last_of_expert = jnp.logical_or(tile_e != next_tile_e, tidx + 1 >= n_active)
    fill = jnp.logical_or(jnp.logical_not(active), last_of_expert).astype(jnp.int32)

    xs = _dispatch(pos, fill, mx, n_rows, tm, top_k, tme)
    ys = _expert_ffn(tile_w, tile_next.astype(jnp.int32), tile_x, flags, xs,
                     w_gate.reshape(-1, d, w_gate.shape[-1]), w_up.reshape(-1, d, w_up.shape[-1]),
                     w_down.reshape(-1, w_down.shape[-2], d), tme)
    return _combine(pos, ys, h, modl, wf, tiles, n_tiles, off, top_k)


def _rope_tables(n_ctx, n_lat, head_dim):
    rows = n_lat // GRID_W
    row_id = jnp.repeat(jnp.arange(rows, dtype=F32), GRID_W)
    col_id = jnp.tile(jnp.arange(GRID_W, dtype=F32), rows)
    half = head_dim // 2
    inv_freq = ROPE_THETA ** (-jnp.arange(0, half, 2, dtype=F32) / half)
    ang_r = row_id[:, None] * inv_freq[None, :]
    ang_c = col_id[:, None] * inv_freq[None, :]
    ang = jnp.concatenate([ang_r, ang_r, ang_c, ang_c], axis=-1)
    reps = HEAD_LANES // head_dim
    cos = jnp.tile(jnp.cos(ang), (1, reps))
    sin = jnp.tile(jnp.sin(ang), (1, reps))
    quarter = head_dim // 4
    first = (jnp.arange(HEAD_LANES) % (2 * quarter)) < quarter
    s_dn = jnp.where(first[None, :], -sin, 0.0)
    s_up = jnp.where(first[None, :], 0.0, sin)
    pad = lambda t, fill: jnp.concatenate([jnp.full((n_ctx, HEAD_LANES), fill, F32), t], axis=0)
    return pad(cos, 1.0), pad(s_dn, 0.0), pad(s_up, 0.0)


def kernel(x, c, ctx, c_ctx, ada_w, ada_b, norm1_g, norm2_g, gmlp_w_in, gmlp_v_gain, gmlp_w_s, gmlp_b_s, gmlp_w_out, gqa_w_qkv, gqa_q_gain, gqa_k_gain, gqa_w_o, diff_w_qkv, diff_q_gain, diff_k_gain, diff_lam_q1, diff_lam_k1, diff_lam_q2, diff_lam_k2, diff_sub_gain, diff_w_o, moe_w_grp, moe_b_grp, moe_w_exp, moe_b_exp, moe_w_gate, moe_w_up, moe_w_down):
    n_batch, n_lat, d = x.shape
    n_ctx = ctx.shape[1]
    depth = ada_w.shape[0]
    tm = 256 if (n_ctx % 256 == 0 and n_lat % 256 == 0) else 128
    assert n_ctx % tm == 0 and n_lat % tm == 0 and tm % CHUNK == 0 and d % LANES == 0
    ctx_tiles, lat_tiles = n_ctx // tm, n_lat // tm

    mod_rows = -(-(n_batch + 1) // 8) * 8
    cc = jnp.zeros((mod_rows, d), F32).at[:n_batch].set(c).at[n_batch].set(c_ctx)
    mod = _ada_table(cc, ada_w, ada_b)
    mod = mod.reshape(depth, mod_rows, 6, 1, d).transpose(0, 2, 1, 3, 4)

    gqa_heads = gqa_w_o.shape[1] // HEAD_LANES
    gqa_kv_heads = (gqa_w_qkv.shape[2] // HEAD_LANES - gqa_heads) // 2
    diff_heads = diff_w_o.shape[1] // HEAD_LANES
    rope_b = _rope_tables(n_ctx, n_lat, HEAD_LANES)
    rope_c = _rope_tables(n_ctx, n_lat, HEAD_LANES // 2)

    h = None
    has_ctx = True
    for i in range(depth):
        kind, j = i % N_MIXERS, i // N_MIXERS
        need_ctx = any((k % N_MIXERS) != 0 for k in range(i + 1, depth))
        modl = mod[i]
        g1 = norm1_g[i].reshape(1, d)
        g2 = norm2_g[i].reshape(1, d)
        in_tiles = _Tiles(n_batch, tm, ctx_tiles if has_ctx else 0)
        n_in = (ctx_tiles if has_ctx else 0) + lat_tiles
        keep = has_ctx and need_ctx
        out_tiles = _Tiles(n_batch, tm, ctx_tiles if keep else 0)
        n_out = (ctx_tiles if keep else 0) + lat_tiles
        off = n_in - n_out

        if kind == 0:
            srcs = (ctx, x) if h is None else (h,)
            h = _gmlp_layer(srcs, modl, g1, gmlp_w_in[j].astype(BF16), gmlp_v_gain[j].reshape(1, -1),
                            gmlp_w_s[j].astype(BF16), gmlp_b_s[j].T, gmlp_w_out[j].astype(BF16),
                            out_tiles, n_out, off)
        else:
            if h is None:
                h = jnp.concatenate([ctx, x], axis=1)
            if kind == 1:
                q, k, v = _qkv_proj(
                    h, modl, g1, gqa_w_qkv[j].astype(BF16), gqa_q_gain[j].reshape(1, -1),
                    gqa_k_gain[j].reshape(1, -1), rope_b if has_ctx else tuple(t[n_ctx:] for t in rope_b),
                    in_tiles, n_in, n_q=gqa_heads, n_k=gqa_kv_heads, n_v=gqa_kv_heads, halves=1,
                    quarter=HEAD_LANES // 4, q_scale=LOG2E / math.sqrt(HEAD_LANES))
                o = _gqa_attention(q, k, v, tm, n_ctx if has_ctx else 0, n_out, off)
                w_o = gqa_w_o[j]
            else:
                lam_init = 0.8 - 0.6 * math.exp(-0.3 * i)
                q, k, v = _qkv_proj(
                    h, modl, g1, diff_w_qkv[j].astype(BF16), diff_q_gain[j].reshape(1, -1),
                    diff_k_gain[j].reshape(1, -1), rope_c if has_ctx else tuple(t[n_ctx:] for t in rope_c),
                    in_tiles, n_in, n_q=diff_heads, n_k=diff_heads, n_v=diff_heads, halves=2,
                    quarter=HEAD_LANES // 8, q_scale=LOG2E / math.sqrt(HEAD_LANES // 2))
                lam_vecs = [t[j].reshape(1, -1) for t in (diff_lam_q1, diff_lam_k1, diff_lam_q2, diff_lam_k2)]
                o = _diff_attention(q, k, v, lam_vecs, diff_sub_gain[j].reshape(1, -1), lam_init, tm,
                                    n_ctx if has_ctx else 0, n_out, off)
                w_o = diff_w_o[j]
            h = _out_proj(o, w_o.astype(BF16), h, modl, out_tiles, n_out, off)

        h = _moe_layer(i, h, modl, g2, moe_w_grp[i], moe_b_grp[i], moe_w_exp[i], moe_b_exp[i],
                       moe_w_gate, moe_w_up, moe_w_down, out_tiles, n_out, 0)
        has_ctx = keep
    return h[:, n_ctx:] if has_ctx else h
```
